```python
import math
import jax, jax.numpy as jnp
from jax import lax
import numpy as np

D_MODEL = 1024
BATCH = 32
SEQ = 256
DEPTH = 4
DEC_BATCH = 8
DEC_SEQ = 1024
PAST_LEN = 512

GRID_W = 64
N_MIXERS = 3
N_ATTN = (DEPTH + 2) // 3
N_MLSTM = (DEPTH + 1) // 3
N_CONV = DEPTH // 3
N_HEADS = 16
N_KV_HEADS = 4
HEAD_DIM = D_MODEL // N_HEADS
Q_BLOCK = 128
ROPE_THETA = 10000.0
MLSTM_INNER = 2 * D_MODEL
MLSTM_HEADS = 4
MLSTM_DH = MLSTM_INNER // MLSTM_HEADS
MLSTM_QKV_BLOCK = 4
MLSTM_CONV = 5
MLSTM_CHUNK = 64
CONV_WIDTH = 31
N_EXPERTS = 16
EXPERT_FF = D_MODEL
CAPACITY_FACTOR = 2
EPS = 1e-6

kernel_name = 'hybrid_diffusion_attn_mlstm_conformer_ecmoe_step'


def rms_norm(x, g):
    xf = x.astype(jnp.float32)
    y = xf * lax.rsqrt(jnp.mean(xf * xf, axis=-1, keepdims=True) + EPS)
    return (y * g.astype(jnp.float32)).astype(x.dtype)


def layer_norm(x, g, b):
    xf = x.astype(jnp.float32)
    mu = jnp.mean(xf, axis=-1, keepdims=True)
    xc = xf - mu
    y = xc * lax.rsqrt(jnp.mean(xc * xc, axis=-1, keepdims=True) + EPS)
    return (y * g.astype(jnp.float32) + b.astype(jnp.float32)).astype(x.dtype)


def ada_modulation(cond, w, b):
    m = jax.nn.silu(cond) @ w + b
    return jnp.split(m[:, None, :], 6, axis=-1)


def modulate(x, g, shift, scale):
    return rms_norm(x, g) * (1 + scale) + shift


def axial_rope_tables(n):
    rows = n // GRID_W
    row = jnp.repeat(jnp.arange(rows), GRID_W).astype(jnp.float32)
    col = jnp.tile(jnp.arange(GRID_W), rows).astype(jnp.float32)
    nf = HEAD_DIM // 4
    inv = ROPE_THETA ** (-jnp.arange(nf, dtype=jnp.float32) / nf)
    ang = jnp.stack([row[:, None] * inv, col[:, None] * inv], axis=1)
    return jnp.cos(ang), jnp.sin(ang)


def apply_rope(x, cos, sin):
    B, n, hx, _ = x.shape
    nf = HEAD_DIM // 4
    xr = x.astype(jnp.float32).reshape(B, n, hx, 2, 2, nf)
    x1, x2 = xr[..., 0, :], xr[..., 1, :]
    c, s = cos[None, :, None], sin[None, :, None]
    out = jnp.stack([x1 * c - x2 * s, x2 * c + x1 * s], axis=-2)
    return out.reshape(x.shape).astype(x.dtype)


def attn_project(h, wq, wk, wv, gq, gk):
    B, N, _ = h.shape
    q = rms_norm((h @ wq).reshape(B, N, N_HEADS, HEAD_DIM), gq)
    k = rms_norm((h @ wk).reshape(B, N, N_KV_HEADS, HEAD_DIM), gk)
    v = (h @ wv).reshape(B, N, N_KV_HEADS, HEAD_DIM)
    return q, k, v


def blocked_attention(q, k, v):
    B, N, _, _ = q.shape
    G = N_HEADS // N_KV_HEADS
    nb = N // Q_BLOCK
    qb = q.reshape(B, nb, Q_BLOCK, N_KV_HEADS, G, HEAD_DIM).transpose(1, 0, 2, 3, 4, 5)
    scale = HEAD_DIM ** -0.5

    def one_block(qblk):
        s = jnp.einsum('bqkgd,btkd->bkgqt', qblk, k).astype(jnp.float32) * scale
        p = jax.nn.softmax(s, axis=-1).astype(v.dtype)
        return jnp.einsum('bkgqt,btkd->bqkgd', p, v)

    o = lax.map(one_block, qb)
    return o.transpose(1, 0, 2, 3, 4, 5).reshape(B, N, N_HEADS * HEAD_DIM)


def depthwise_conv(x, w):
    K = w.shape[0]
    return lax.conv_general_dilated(x, w[:, None, :].astype(x.dtype), window_strides=(1,),
                                    padding=[(K // 2, K // 2)],
                                    dimension_numbers=('NWC', 'WIO', 'NWC'),
                                    feature_group_count=x.shape[-1])


def headwise(x, w):
    B, T, C = x.shape
    xb = x.reshape(B, T, C // MLSTM_QKV_BLOCK, MLSTM_QKV_BLOCK)
    return jnp.einsum('btni,nio->btno', xb, w).reshape(B, T, C)


def mlstm_scan(q, k, v, i_pre, f_pre, C0, n0, m0):
    B, H, T, dh = q.shape
    L = MLSTM_CHUNK
    nc = T // L
    f32 = jnp.float32

    def chunks(a):
        return jnp.moveaxis(a.astype(f32).reshape(B, H, nc, L, *a.shape[3:]), 2, 0)

    mask = jnp.tril(jnp.ones((L, L), dtype=bool))

    def step(carry, xs):
        C, n, m = carry
        qc, kc, vc, ic, fc = xs
        b = jnp.cumsum(jax.nn.log_sigmoid(fc), axis=-1)
        dmat = jnp.where(mask, b[..., :, None] - b[..., None, :] + ic[..., None, :], -jnp.inf)
        inter = b + m[..., None]
        mt = jnp.maximum(inter, jnp.max(dmat, axis=-1))
        w_in = jnp.exp(inter - mt)
        s = jnp.einsum('bhtd,bhsd->bhts', qc, kc) * jnp.exp(dmat - mt[..., None])
        num = w_in[..., None] * jnp.einsum('bhvd,bhtd->bhtv', C, qc) + jnp.einsum('bhts,bhsv->bhtv', s, vc)
        den = w_in * jnp.einsum('bhd,bhtd->bht', n, qc) + jnp.sum(s, axis=-1)
        h = num / jnp.maximum(jnp.abs(den), jnp.exp(-mt))[..., None]
        m_new = mt[..., -1]
        w_state = jnp.exp(b[..., -1] + m - m_new)
        w_s = jnp.exp(b[..., -1:] - b + ic - m_new[..., None])
        C_new = w_state[..., None, None] * C + jnp.einsum('bhs,bhsv,bhsd->bhvd', w_s, vc, kc)
        n_new = w_state[..., None] * n + jnp.einsum('bhs,bhsd->bhd', w_s, kc)
        return (C_new, n_new, m_new), h

    init = (C0.astype(f32), n0.astype(f32), m0.astype(f32))
    (C, n, m), hs = lax.scan(step, init, (chunks(q), chunks(k), chunks(v), chunks(i_pre), chunks(f_pre)))
    h = jnp.moveaxis(hs, 0, 2).reshape(B, H, T, dh)
    return h, (C, n, m)


def mlstm_mixer(h, w_up, w_conv, w_q, w_k, w_v, w_if, b_if, w_o, g_out, skip, w_down, C0, n0, m0):
    B, T, _ = h.shape
    H, dh = MLSTM_HEADS, MLSTM_DH
    xm, z = jnp.split(h @ w_up, 2, axis=-1)
    xc = jax.nn.silu(depthwise_conv(xm, w_conv))
    q = headwise(xc, w_q)
    k = headwise(xc, w_k) * (dh ** -0.5)
    v = headwise(xm, w_v)
    qkv = jnp.concatenate([q, k, v], axis=-1)

    def to_heads(a):
        return a.reshape(B, T, H, dh).transpose(0, 2, 1, 3)

    qh, kh, vh = to_heads(q), to_heads(k), to_heads(v)
    outs, finals = [], []
    for d in range(2):
        gates = (qkv @ w_if[d] + b_if[d]).astype(jnp.float32)
        i_pre = gates[..., :H].transpose(0, 2, 1)
        f_pre = gates[..., H:].transpose(0, 2, 1)
        seqs = (qh, kh, vh, i_pre, f_pre)
        if d == 1:
            seqs = tuple(jnp.flip(a, axis=2) for a in seqs)
        hd, st = mlstm_scan(*seqs, C0[:, d], n0[:, d], m0[:, d])
        if d == 1:
            hd = jnp.flip(hd, axis=2)
        o = jax.nn.sigmoid(h @ w_o[d])
        outs.append(o * hd.transpose(0, 2, 1, 3).reshape(B, T, MLSTM_INNER).astype(h.dtype))
        finals.append(st)
    hsum = outs[0] + outs[1]
    hn = rms_norm(hsum.reshape(B, T, H, dh), g_out.reshape(H, dh)).reshape(B, T, MLSTM_INNER)
    y = (hn + skip * xc) * jax.nn.silu(z)
    C = jnp.stack([finals[0][0], finals[1][0]], axis=1)
    n = jnp.stack([finals[0][1], finals[1][1]], axis=1)
    m = jnp.stack([finals[0][2], finals[1][2]], axis=1)
    return y @ w_down, (C, n, m)


def conformer_conv(h, w_pw1, b_pw1, w_dw, b_dw, g_ln, b_ln, w_pw2, b_pw2):
    a, g = jnp.split(h @ w_pw1 + b_pw1, 2, axis=-1)
    u = a * jax.nn.sigmoid(g)
    u = depthwise_conv(u, w_dw) + b_dw
    u = jax.nn.silu(layer_norm(u, g_ln, b_ln))
    return u @ w_pw2 + b_pw2


def expert_choice_ffn(x, w_router, w_gate, w_up, w_down):
    B, N, D = x.shape
    cap = CAPACITY_FACTOR * N // N_EXPERTS
    aff = jax.nn.softmax((x @ w_router).astype(jnp.float32), axis=-1)
    vals, idx = lax.top_k(aff.transpose(0, 2, 1), cap)
    xs = jax.vmap(lambda xb, ib: xb[ib])(x, idx)
    hg = jnp.einsum('becd,edf->becf', xs, w_gate)
    hu = jnp.einsum('becd,edf->becf', xs, w_up)
    ys = jnp.einsum('becf,efd->becd', jax.nn.silu(hg) * hu, w_down) * vals[..., None].astype(x.dtype)
    return jax.vmap(lambda ib, yb: jnp.zeros((N, D), x.dtype).at[ib.reshape(-1)].add(yb.reshape(-1, D)))(idx, ys)


def setup_inputs(seed: int = 0) -> dict:
    key = jax.random.key(seed)
    ks = iter(jax.random.split(key, 64))

    def nrm(shape, scale=1.0):
        return scale * jax.random.normal(next(ks), shape, jnp.float32)

    D, INNER, H, dh = D_MODEL, MLSTM_INNER, MLSTM_HEADS, MLSTM_DH
    nblk = INNER // MLSTM_QKV_BLOCK
    inp = {}
    inp['x_prompt'] = nrm((BATCH, SEQ, D))
    inp['x_sample'] = nrm((DEC_BATCH, DEC_SEQ, D))
    inp['cache_k'] = nrm((DEC_BATCH, N_ATTN, PAST_LEN, N_KV_HEADS, HEAD_DIM))
    inp['cache_v'] = nrm((DEC_BATCH, N_ATTN, PAST_LEN, N_KV_HEADS, HEAD_DIM), 0.5)
    inp['state_C'] = nrm((DEC_BATCH, N_MLSTM, 2, H, dh, dh), 0.02)
    inp['state_n'] = nrm((DEC_BATCH, N_MLSTM, 2, H, dh), 0.1)
    inp['state_m'] = nrm((DEC_BATCH, N_MLSTM, 2, H), 0.5)
    inp['c'] = nrm((DEC_BATCH, D))
    inp['c_ctx'] = nrm((D,))
    inp['w_ada'] = nrm((DEPTH, D, 6 * D), 0.5 * D ** -0.5)
    inp['b_ada'] = nrm((DEPTH, 6 * D), 0.02)
    inp['g_norm1'] = 1.0 + nrm((DEPTH, D), 0.02)
    inp['g_norm2'] = 1.0 + nrm((DEPTH, D), 0.02)
    inp['attn_wq'] = nrm((N_ATTN, D, N_HEADS * HEAD_DIM), D ** -0.5)
    inp['attn_wk'] = nrm((N_ATTN, D, N_KV_HEADS * HEAD_DIM), D ** -0.5)
    inp['attn_wv'] = nrm((N_ATTN, D, N_KV_HEADS * HEAD_DIM), D ** -0.5)
    inp['attn_wo'] = nrm((N_ATTN, N_HEADS * HEAD_DIM, D), (N_HEADS * HEAD_DIM) ** -0.5)
    inp['attn_gq'] = 1.0 + nrm((N_ATTN, HEAD_DIM), 0.02)
    inp['attn_gk'] = 1.0 + nrm((N_ATTN, HEAD_DIM), 0.02)
    inp['ml_w_up'] = nrm((N_MLSTM, D, 2 * INNER), D ** -0.5)
    inp['ml_conv'] = nrm((N_MLSTM, MLSTM_CONV, INNER), MLSTM_CONV ** -0.5)
    inp['ml_wq'] = nrm((N_MLSTM, nblk, MLSTM_QKV_BLOCK, MLSTM_QKV_BLOCK), MLSTM_QKV_BLOCK ** -0.5)
    inp['ml_wk'] = nrm((N_MLSTM, nblk, MLSTM_QKV_BLOCK, MLSTM_QKV_BLOCK), MLSTM_QKV_BLOCK ** -0.5)
    inp['ml_wv'] = nrm((N_MLSTM, nblk, MLSTM_QKV_BLOCK, MLSTM_QKV_BLOCK), MLSTM_QKV_BLOCK ** -0.5)
    inp['ml_w_if'] = nrm((N_MLSTM, 2, 3 * INNER, 2 * H), (3 * INNER) ** -0.5)
    b_i = nrm((N_MLSTM, 2, H), 0.1)
    b_f = 3.0 + 3.0 * jax.random.uniform(next(ks), (N_MLSTM, 2, H), jnp.float32)
    inp['ml_b_if'] = jnp.concatenate([b_i, b_f], axis=-1)
    inp['ml_w_o'] = nrm((N_MLSTM, 2, D, INNER), D ** -0.5)
    inp['ml_g_out'] = 1.0 + nrm((N_MLSTM, INNER), 0.02)
    inp['ml_skip'] = 1.0 + nrm((N_MLSTM, INNER), 0.02)
    inp['ml_w_down'] = nrm((N_MLSTM, INNER, D), INNER ** -0.5)
    inp['cv_w_pw1'] = nrm((N_CONV, D, 2 * D), D ** -0.5)
    inp['cv_b_pw1'] = nrm((N_CONV, 2 * D), 0.02)
    inp['cv_w_dw'] = nrm((N_CONV, CONV_WIDTH, D), CONV_WIDTH ** -0.5)
    inp['cv_b_dw'] = nrm((N_CONV, D), 0.02)
    inp['cv_g_ln'] = 1.0 + nrm((N_CONV, D), 0.02)
    inp['cv_b_ln'] = nrm((N_CONV, D), 0.02)
    inp['cv_w_pw2'] = nrm((N_CONV, D, D), D ** -0.5)
    inp['cv_b_pw2'] = nrm((N_CONV, D), 0.02)
    inp['moe_router'] = nrm((DEPTH, D, N_EXPERTS), D ** -0.5)
    inp['moe_w_gate'] = nrm((DEPTH, N_EXPERTS, D, EXPERT_FF), D ** -0.5)
    inp['moe_w_up'] = nrm((DEPTH, N_EXPERTS, D, EXPERT_FF), D ** -0.5)
    inp['moe_w_down'] = nrm((DEPTH, N_EXPERTS, EXPERT_FF, D), EXPERT_FF ** -0.5)
    inp['g_final'] = 1.0 + nrm((D,), 0.02)
    return inp


def reference(x_prompt, x_sample, cache_k, cache_v, state_C, state_n, state_m, c, c_ctx,
              w_ada, b_ada, g_norm1, g_norm2,
              attn_wq, attn_wk, attn_wv, attn_wo, attn_gq, attn_gk,
              ml_w_up, ml_conv, ml_wq, ml_wk, ml_wv, ml_w_if, ml_b_if, ml_w_o, ml_g_out, ml_skip, ml_w_down,
              cv_w_pw1, cv_b_pw1, cv_w_dw, cv_b_dw, cv_g_ln, cv_b_ln, cv_w_pw2, cv_b_pw2,
              moe_router, moe_w_gate, moe_w_up, moe_w_down, g_final):
    Bp = x_prompt.shape[0]
    N = x_sample.shape[1]
    cos, sin = axial_rope_tables(N)
    H, dh = MLSTM_HEADS, MLSTM_DH
    zC = jnp.zeros((Bp, 2, H, dh, dh), jnp.float32)
    zn = jnp.zeros((Bp, 2, H, dh), jnp.float32)
    zm = jnp.zeros((Bp, 2, H), jnp.float32)
    cond_ctx = c_ctx[None]
    xp, xs = x_prompt, x_sample
    new_k, new_v, new_C, new_n, new_m = [], [], [], [], []
    for layer in range(DEPTH):
        kind = layer % N_MIXERS
        j = layer // N_MIXERS
        sp1, cp1, gp1, sp2, cp2, gp2 = ada_modulation(cond_ctx, w_ada[layer], b_ada[layer])
        ss1, cs1, gs1, ss2, cs2, gs2 = ada_modulation(c, w_ada[layer], b_ada[layer])
        hp = modulate(xp, g_norm1[layer], sp1, cp1)
        hs = modulate(xs, g_norm1[layer], ss1, cs1)
        if kind == 0:
            wts = (attn_wq[j], attn_wk[j], attn_wv[j], attn_gq[j], attn_gk[j])
            qp, kp, vp = attn_project(hp, *wts)
            op = blocked_attention(qp, kp, vp) @ attn_wo[j]
            qs, ks_, vs = attn_project(hs, *wts)
            qs, ks_ = apply_rope(qs, cos, sin), apply_rope(ks_, cos, sin)
            k_all = jnp.concatenate([ks_, cache_k[:, j].astype(ks_.dtype)], axis=1)
            v_all = jnp.concatenate([vs, cache_v[:, j].astype(vs.dtype)], axis=1)
            os_ = blocked_attention(qs, k_all, v_all) @ attn_wo[j]
            new_k.append(kp)
            new_v.append(vp)
        elif kind == 1:
            wts = (ml_w_up[j], ml_conv[j], ml_wq[j], ml_wk[j], ml_wv[j], ml_w_if[j], ml_b_if[j],
                   ml_w_o[j], ml_g_out[j], ml_skip[j], ml_w_down[j])
            op, (Cp, np_, mp) = mlstm_mixer(hp, *wts, zC, zn, zm)
            os_, _ = mlstm_mixer(hs, *wts, state_C[:, j], state_n[:, j], state_m[:, j])
            new_C.append(Cp)
            new_n.append(np_)
            new_m.append(mp)
        else:
            wts = (cv_w_pw1[j], cv_b_pw1[j], cv_w_dw[j], cv_b_dw[j], cv_g_ln[j], cv_b_ln[j],
                   cv_w_pw2[j], cv_b_pw2[j])
            op = conformer_conv(hp, *wts)
            os_ = conformer_conv(hs, *wts)
        xp = xp + gp1 * op
        xs = xs + gs1 * os_
        moe_w = (moe_router[layer], moe_w_gate[layer], moe_w_up[layer], moe_w_down[layer])
        xp = xp + gp2 * expert_choice_ffn(modulate(xp, g_norm2[layer], sp2, cp2), *moe_w)
        xs = xs + gs2 * expert_choice_ffn(modulate(xs, g_norm2[layer], ss2, cs2), *moe_w)
    y_prompt = rms_norm(xp, g_final)
    y_sample = rms_norm(xs, g_final)
    new_cache_k = jnp.stack(new_k, axis=1)
    new_cache_v = jnp.stack(new_v, axis=1)
    new_state_C = jnp.stack(new_C, axis=1)
    new_state_n = jnp.stack(new_n, axis=1)
    new_state_m = jnp.stack(new_m, axis=1)
    return (y_prompt, y_sample, new_cache_k, new_cache_v, new_state_C, new_state_n, new_state_m)
```

```python
import functools
import math

import jax
import jax.numpy as jnp
from jax import lax
from jax.experimental import pallas as pl
from jax.experimental.pallas import tpu as pltpu

F32 = jnp.float32
BF16 = jnp.bfloat16

D = 1024
N_PROMPT_REQ, PROMPT_LEN = 32, 256
N_SAMPLE_REQ, SAMPLE_LEN = 8, 1024
N_PROMPT_TOK = N_PROMPT_REQ * PROMPT_LEN
N_TOK = N_PROMPT_TOK + N_SAMPLE_REQ * SAMPLE_LEN
DEPTH = 4
GRID_W = 64
N_HEADS, N_KV, HEAD_DIM = 16, 4, 64
PAST_LEN = 512
ROPE_THETA = 10000.0
ML_INNER, ML_HEADS, ML_DH = 2048, 4, 512
ML_CONV, ML_CHUNK = 5, 256
CONV_WIDTH = 31
CONV_HALO = 16
N_EXPERTS = 16
CAP_PROMPT = 2 * PROMPT_LEN // N_EXPERTS
CAP_SAMPLE = 2 * SAMPLE_LEN // N_EXPERTS
ROWS_PER_EXPERT = N_PROMPT_REQ * CAP_PROMPT + N_SAMPLE_REQ * CAP_SAMPLE
EPS = 1e-6
LANES = 128
VMEM_LIMIT = 56 * 1024 * 1024

PROMPT = dict(nreq=N_PROMPT_REQ, n=PROMPT_LEN, cap=CAP_PROMPT, tok0=0, slot0=0)
SAMPLE = dict(nreq=N_SAMPLE_REQ, n=SAMPLE_LEN, cap=CAP_SAMPLE, tok0=N_PROMPT_TOK,
              slot0=N_PROMPT_REQ * CAP_PROMPT)


def _params(n_axes):
    return pltpu.CompilerParams(dimension_semantics=("arbitrary",) * n_axes,
                                vmem_limit_bytes=VMEM_LIMIT)


def _cond_index(i, tm):
    npt = N_PROMPT_TOK // tm
    return jnp.where(i < npt, 0, 1 + (i - npt) // (SAMPLE_LEN // tm))


def _sigmoid(x):
    return 1.0 / (1.0 + jnp.exp(-x))


def _silu(x):
    return x * _sigmoid(x)


def _dot(a, b):
    return jnp.dot(a, b, preferred_element_type=F32)


def _dot_nt(a, b):
    return lax.dot_general(a, b, (((1,), (1,)), ((), ())), preferred_element_type=F32)


def _dot_tn(a, b):
    return lax.dot_general(a, b, (((0,), (0,)), ((), ())), preferred_element_type=F32)


def _split3(x):
    p1 = x.astype(BF16)
    r1 = x - p1.astype(F32)
    p2 = r1.astype(BF16)
    p3 = (r1 - p2.astype(F32)).astype(BF16)
    return p1, p2, p3


def _modulated_norm(x, g, shift, scale):
    ms = jnp.mean(x * x, axis=1, keepdims=True)
    return x * lax.rsqrt(ms + EPS) * g * (1.0 + scale) + shift


def _ada_kernel(c_ref, w_ref, b_ref, o_ref):
    s = _silu(c_ref[...]).astype(BF16)
    o_ref[0] = _dot(s, w_ref[0].astype(BF16)) + b_ref[0]


def _ada_table(cond, w_ada, b_ada):
    return pl.pallas_call(
        _ada_kernel,
        grid=(DEPTH, 6),
        in_specs=[pl.BlockSpec((16, D), lambda l, j: (0, 0)),
                  pl.BlockSpec((1, D, D), lambda l, j: (l, 0, j)),
                  pl.BlockSpec((1, 1, D), lambda l, j: (l, 0, j))],
        out_specs=pl.BlockSpec((1, 16, D), lambda l, j: (l, 0, j)),
        out_shape=jax.ShapeDtypeStruct((DEPTH, 16, 6 * D), F32),
        compiler_params=_params(2), name="ada_table",
    )(cond, w_ada, b_ada.reshape(DEPTH, 1, 6 * D))


def _prenorm_kernel(x_ref, mod_ref, g_ref, h_ref):
    m = mod_ref[0]
    h_ref[...] = _modulated_norm(x_ref[...], g_ref[...], m[0:1], m[1:2]).astype(BF16)


def _prenorm(x, mod, g):
    tm = 512
    return pl.pallas_call(
        _prenorm_kernel,
        grid=(N_TOK // tm,),
        in_specs=[pl.BlockSpec((tm, D), lambda i: (i, 0)),
                  pl.BlockSpec((1, 6, D), lambda i: (_cond_index(i, tm), 0, 0)),
                  pl.BlockSpec((1, D), lambda i: (0, 0))],
        out_specs=pl.BlockSpec((tm, D), lambda i: (i, 0)),
        out_shape=jax.ShapeDtypeStruct((N_TOK, D), BF16),
        compiler_params=_params(1), name="prenorm",
    )(x, mod, g.reshape(1, D))


def _mixer_epilogue(o, x_ref, mod_ref, g2_ref, wr_ref, xo_ref, h2_ref, aff_ref):
    m = mod_ref[0]
    x_new = x_ref[...] + m[2:3] * o
    xo_ref[...] = x_new
    h2 = _modulated_norm(x_new, g2_ref[...], m[3:4], m[4:5])
    hh = h2.astype(BF16)
    h2_ref[...] = hh
    hl = (h2 - hh.astype(F32)).astype(BF16)
    wr = wr_ref[...]
    wh = wr.astype(BF16)
    wl = (wr - wh.astype(F32)).astype(BF16)
    logits = _dot(hh, wh) + _dot(hh, wl) + _dot(hl, wh)
    lt = logits.T[:N_EXPERTS]
    e = jnp.exp(lt - jnp.max(lt, axis=0, keepdims=True))
    aff_ref[...] = e / jnp.sum(e, axis=0, keepdims=True)


def _epilogue_specs(tm):
    in_specs = [pl.BlockSpec((tm, D), lambda i: (i, 0)),
                pl.BlockSpec((1, 6, D), lambda i: (_cond_index(i, tm), 0, 0)),
                pl.BlockSpec((1, D), lambda i: (0, 0)),
                pl.BlockSpec((D, LANES), lambda i: (0, 0))]
    out_specs = [pl.BlockSpec((tm, D), lambda i: (i, 0)),
                 pl.BlockSpec((tm, D), lambda i: (i, 0)),
                 pl.BlockSpec((N_EXPERTS, tm), lambda i: (0, i))]
    out_shape = [jax.ShapeDtypeStruct((N_TOK, D), F32),
                 jax.ShapeDtypeStruct((N_TOK, D), BF16),
                 jax.ShapeDtypeStruct((N_EXPERTS, N_TOK), F32)]
    return in_specs, out_specs, out_shape


def _dense_out_kernel(*refs, layer_norm):
    if layer_norm:
        a_ref, w_ref, b_ref, gl_ref, bl_ref = refs[:5]
        rest = refs[5:]
        u = a_ref[...]
        mu = jnp.mean(u, axis=1, keepdims=True)
        uc = u - mu
        y = uc * lax.rsqrt(jnp.mean(uc * uc, axis=1, keepdims=True) + EPS)
        a = _silu(y * gl_ref[...] + bl_ref[...]).astype(BF16)
        o = _dot(a, w_ref[...]) + b_ref[...]
    else:
        a_ref, w_ref = refs[:2]
        rest = refs[2:]
        o = _dot(a_ref[...], w_ref[...])
    _mixer_epilogue(o, *rest)


def _dense_out(a, w, x, mod, g2, wr, bias=None, ln=None):
    tm = 512
    e_in, e_out, e_shape = _epilogue_specs(tm)
    in_specs = [pl.BlockSpec((tm, D), lambda i: (i, 0)), pl.BlockSpec((D, D), lambda i: (0, 0))]
    args = [a, w]
    if ln is not None:
        in_specs += [pl.BlockSpec((1, D), lambda i: (0, 0))] * 3
        args += [bias.reshape(1, D), ln[0].reshape(1, D), ln[1].reshape(1, D)]
    return pl.pallas_call(
        functools.partial(_dense_out_kernel, layer_norm=ln is not None),
        grid=(N_TOK // tm,),
        in_specs=in_specs + e_in, out_specs=e_out, out_shape=e_shape,
        compiler_params=_params(1), name="dense_out",
    )(*args, x, mod, g2.reshape(1, D), wr)


def _head_rms(x, gain):
    lo = lax.broadcasted_iota(jnp.int32, (1, LANES), 1) < HEAD_DIM
    outs = []
    for j in range(x.shape[1] // LANES):
        seg = x[:, LANES * j:LANES * (j + 1)]
        s2 = seg * seg
        a = jnp.sum(jnp.where(lo, s2, 0.0), axis=1, keepdims=True)
        b = jnp.sum(jnp.where(lo, 0.0, s2), axis=1, keepdims=True)
        r = jnp.where(lo, lax.rsqrt(a * (1.0 / HEAD_DIM) + EPS), lax.rsqrt(b * (1.0 / HEAD_DIM) + EPS))
        outs.append(seg * r)
    return jnp.concatenate(outs, axis=1) * gain


def _rope(x, c, s):
    first = (lax.broadcasted_iota(jnp.int32, (1, LANES), 1) & 31) < 16
    outs = []
    for j in range(x.shape[1] // LANES):
        sl = slice(LANES * j, LANES * (j + 1))
        seg = x[:, sl]
        partner = jnp.where(first, pltpu.roll(seg, LANES - 16, 1), pltpu.roll(seg, 16, 1))
        outs.append(seg * c[:, sl] + partner * s[:, sl])
    return jnp.concatenate(outs, axis=1)


def _qkv_kernel(h_ref, w_ref, gq_ref, gk_ref, ct_ref, st_ref, q_ref, k_ref, v_ref, kt_ref, vt_ref):
    qkv = _dot(h_ref[...], w_ref[...])
    nq, nk = N_HEADS * HEAD_DIM, N_KV * HEAD_DIM
    q = _head_rms(qkv[:, :nq], gq_ref[...])
    k = _head_rms(qkv[:, nq:nq + nk], gk_ref[...])
    v = qkv[:, nq + nk:]
    kt_ref[...] = k
    vt_ref[...] = v
    c, s = ct_ref[...], st_ref[...]
    q = _rope(q, c, s) * (HEAD_DIM ** -0.5)
    k = _rope(k, c[:, :nk], s[:, :nk])
    for hd in range(N_HEADS):
        q_ref[hd] = q[:, HEAD_DIM * hd:HEAD_DIM * (hd + 1)].astype(BF16)
    for hd in range(N_KV):
        k_ref[hd] = k[:, HEAD_DIM * hd:HEAD_DIM * (hd + 1)].astype(BF16)
        v_ref[hd] = v[:, HEAD_DIM * hd:HEAD_DIM * (hd + 1)].astype(BF16)


def _rope_tables(tm):
    n = SAMPLE_LEN
    row = jnp.repeat(jnp.arange(n // GRID_W), GRID_W).astype(F32)
    col = jnp.tile(jnp.arange(GRID_W), n // GRID_W).astype(F32)
    nf = HEAD_DIM // 4
    inv = ROPE_THETA ** (-jnp.arange(nf, dtype=F32) / nf)
    ar, ac = row[:, None] * inv, col[:, None] * inv
    cos = jnp.concatenate([jnp.cos(ar), jnp.cos(ar), jnp.cos(ac), jnp.cos(ac)], axis=1)
    sin = jnp.concatenate([-jnp.sin(ar), jnp.sin(ar), -jnp.sin(ac), jnp.sin(ac)], axis=1)
    cos = jnp.concatenate([jnp.ones((tm, HEAD_DIM), F32), cos], axis=0)
    sin = jnp.concatenate([jnp.zeros((tm, HEAD_DIM), F32), sin], axis=0)
    return jnp.tile(cos, (1, N_HEADS)), jnp.tile(sin, (1, N_HEADS))


def _attn_qkv(h, wqkv, gq, gk):
    tm = 512
    npt = N_PROMPT_TOK // tm
    ct, st = _rope_tables(tm)
    nq, nk = N_HEADS * HEAD_DIM, N_KV * HEAD_DIM

    def tab(i):
        return (jnp.where(i < npt, 0, 1 + (i - npt) % (SAMPLE_LEN // tm)), 0)

    return pl.pallas_call(
        _qkv_kernel,
        grid=(N_TOK // tm,),
        in_specs=[pl.BlockSpec((tm, D), lambda i: (i, 0)),
                  pl.BlockSpec((D, nq + 2 * nk), lambda i: (0, 0)),
                  pl.BlockSpec((1, nq), lambda i: (0, 0)),
                  pl.BlockSpec((1, nk), lambda i: (0, 0)),
                  pl.BlockSpec((tm, nq), tab),
                  pl.BlockSpec((tm, nq), tab)],
        out_specs=[pl.BlockSpec((N_HEADS, tm, HEAD_DIM), lambda i: (0, i, 0)),
                   pl.BlockSpec((N_KV, tm, HEAD_DIM), lambda i: (0, i, 0)),
                   pl.BlockSpec((N_KV, tm, HEAD_DIM), lambda i: (0, i, 0)),
                   pl.BlockSpec((tm, nk), lambda i: (i, 0)),
                   pl.BlockSpec((tm, nk), lambda i: (i, 0))],
        out_shape=[jax.ShapeDtypeStruct((N_HEADS, N_TOK, HEAD_DIM), BF16),
                   jax.ShapeDtypeStruct((N_KV, N_TOK, HEAD_DIM), BF16),
                   jax.ShapeDtypeStruct((N_KV, N_TOK, HEAD_DIM), BF16),
                   jax.ShapeDtypeStruct((N_TOK, nk), F32),
                   jax.ShapeDtypeStruct((N_TOK, nk), F32)],
        compiler_params=_params(1), name="attn_qkv",
    )(h, wqkv, jnp.tile(gq, N_HEADS).reshape(1, nq), jnp.tile(gk, N_KV).reshape(1, nk), ct, st)


def _attn_kernel(*refs, tq, has_cache):
    if has_cache:
        q_ref, k_ref, v_ref, ck_ref, cv_ref, _, o_ref = refs
    else:
        q_ref, k_ref, v_ref, o_ref = refs
    qb = 128
    group = N_HEADS // N_KV
    for kv in range(N_KV):
        k = k_ref[kv]
        v = v_ref[kv]
        if has_cache:
            sl = slice(HEAD_DIM * kv, HEAD_DIM * (kv + 1))
            k = jnp.concatenate([k, ck_ref[0, 0, :, sl].astype(BF16)], axis=0)
            v = jnp.concatenate([v, cv_ref[0, 0, :, sl].astype(BF16)], axis=0)

        def block(i, carry, kv=kv, k=k, v=v):
            r0 = pl.multiple_of(i * qb, qb)
            q = q_ref[group * kv:group * (kv + 1), pl.ds(r0, qb), :].reshape(group * qb, HEAD_DIM)
            s = _dot_nt(q, k)
            p = jnp.exp(s - jnp.max(s, axis=1, keepdims=True))
            o = _dot(p.astype(BF16), v) / jnp.sum(p, axis=1, keepdims=True)
            o = jnp.concatenate([o[qb * g:qb * (g + 1)] for g in range(group)], axis=1)
            o_ref[pl.ds(r0, qb), group * HEAD_DIM * kv:group * HEAD_DIM * (kv + 1)] = o.astype(BF16)
            return carry

        lax.fori_loop(0, tq // qb, block, 0)


def _attention(q, k, v, cache_k, cache_v, layer_j):
    outs = None
    for cfg, has_cache in ((PROMPT, False), (SAMPLE, True)):
        tq, off = cfg["n"], cfg["tok0"] // cfg["n"]
        in_specs = [pl.BlockSpec((N_HEADS, tq, HEAD_DIM), lambda b, off=off: (0, off + b, 0)),
                    pl.BlockSpec((N_KV, tq, HEAD_DIM), lambda b, off=off: (0, off + b, 0)),
                    pl.BlockSpec((N_KV, tq, HEAD_DIM), lambda b, off=off: (0, off + b, 0))]
        args = [q, k, v]
        aliases = {}
        if has_cache:
            cspec = pl.BlockSpec((1, 1, PAST_LEN, N_KV * HEAD_DIM), lambda b: (b, layer_j, 0, 0))
            in_specs += [cspec, cspec, pl.BlockSpec(memory_space=pl.ANY)]
            args += [cache_k, cache_v, outs]
            aliases = {5: 0}
        outs = pl.pallas_call(
            functools.partial(_attn_kernel, tq=tq, has_cache=has_cache),
            grid=(cfg["nreq"],),
            in_specs=in_specs,
            out_specs=pl.BlockSpec((tq, D), lambda b, off=off: (off + b, 0)),
            out_shape=jax.ShapeDtypeStruct((N_TOK, D), BF16),
            input_output_aliases=aliases,
            compiler_params=_params(1), name="attention",
        )(*args)
    return outs


def _dwconv(pad_ref, x, w, req_len, width):
    nsub = x.shape[0] // req_len
    zeros = jnp.zeros((CONV_HALO, x.shape[1]), F32)
    for s in range(nsub):
        pad_ref[s, 0:CONV_HALO, :] = zeros
        pad_ref[s, CONV_HALO + req_len:2 * CONV_HALO + req_len, :] = zeros
        pad_ref[s, CONV_HALO:CONV_HALO + req_len, :] = x[s * req_len:(s + 1) * req_len]
    outs = []
    for s in range(nsub):
        acc = None
        for d in range(width):
            tap = pad_ref[s, pl.ds(CONV_HALO + d - width // 2, req_len), :] * w[d:d + 1, :]
            acc = tap if acc is None else acc + tap
        outs.append(acc)
    return outs[0] if nsub == 1 else jnp.concatenate(outs, axis=0)


def _log_sigmoid(x):
    return jnp.minimum(x, 0.0) - jnp.log1p(jnp.exp(-jnp.abs(x)))


def _ml_in_kernel(*refs, req_len, aliased):
    (h_ref, wup_ref, wc_ref, wqkv_ref, wif_ref, bif_ref) = refs[:6]
    outs = refs[6 + (6 if aliased else 0):]
    xc_ref, q_ref, k_ref, v_ref, gcol_ref, grow_ref, pad_ref, acc_ref = outs
    g = pl.program_id(1)
    xm = _dot(h_ref[...], wup_ref[...])
    xc = _silu(_dwconv(pad_ref, xm, wc_ref[...], req_len, ML_CONV))
    xc_ref[...] = xc
    xcb = xc.astype(BF16)
    q = _dot(xcb, wqkv_ref[0, 0]).astype(BF16)
    k = (_dot(xcb, wqkv_ref[0, 1]) * (ML_DH ** -0.5)).astype(BF16)
    v = _dot(xm.astype(BF16), wqkv_ref[0, 2]).astype(BF16)
    q_ref[...] = q
    k_ref[...] = k
    v_ref[...] = v
    part = _dot(q, wif_ref[0, 0]) + _dot(k, wif_ref[1, 0]) + _dot(v, wif_ref[2, 0])

    @pl.when(g == 0)
    def _():
        acc_ref[...] = part

    @pl.when(g > 0)
    def _():
        acc_ref[...] += part

    @pl.when(g == pl.num_programs(1) - 1)
    def _():
        gates = acc_ref[...] + bif_ref[...]
        lf = _log_sigmoid(gates)
        L = ML_CHUNK
        ti = lax.broadcasted_iota(jnp.int32, (L, L), 0)
        ui = lax.broadcasted_iota(jnp.int32, (L, L), 1)
        tri_f = jnp.where(ui <= ti, 1.0, 0.0).astype(BF16)
        tri_b = jnp.where(ui >= ti, 1.0, 0.0).astype(BF16)
        kind = lax.broadcasted_iota(jnp.int32, (1, LANES), 1) & 7
        rows = []
        for c in range(gates.shape[0] // L):
            p1, p2, p3 = _split3(lf[c * L:(c + 1) * L])
            bf = _dot(tri_f, p1) + _dot(tri_f, p2) + _dot(tri_f, p3)
            bb = _dot(tri_b, p1) + _dot(tri_b, p2) + _dot(tri_b, p3)
            rows.append(jnp.where(kind == 1, bf, jnp.where(kind == 3, bb, gates[c * L:(c + 1) * L])))
        out = jnp.concatenate(rows, axis=0)
        gcol_ref[...] = out
        grow_ref[...] = out.T


def _ml_in(h, wup, wconv, wqkv, wif, bif):
    tm, cg = 1024, 256
    ngroups = ML_INNER // cg
    shapes = [jax.ShapeDtypeStruct((N_TOK, ML_INNER), F32),
              jax.ShapeDtypeStruct((N_TOK, ML_INNER), BF16),
              jax.ShapeDtypeStruct((N_TOK, ML_INNER), BF16),
              jax.ShapeDtypeStruct((N_TOK, ML_INNER), BF16),
              jax.ShapeDtypeStruct((N_TOK, LANES), F32),
              jax.ShapeDtypeStruct((LANES, N_TOK), F32)]
    outs = None
    for cfg in (PROMPT, SAMPLE):
        off = cfg["tok0"] // tm
        in_specs = [pl.BlockSpec((tm, D), lambda i, g, off=off: (off + i, 0)),
                    pl.BlockSpec((D, cg), lambda i, g: (0, g)),
                    pl.BlockSpec((ML_CONV, cg), lambda i, g: (0, g)),
                    pl.BlockSpec((1, 3, cg, cg), lambda i, g: (g, 0, 0, 0)),
                    pl.BlockSpec((3, 1, cg, LANES), lambda i, g: (0, g, 0, 0)),
                    pl.BlockSpec((1, LANES), lambda i, g: (0, 0))]
        args = [h, wup, wconv, wqkv, wif, bif]
        aliases = {}
        if outs is not None:
            in_specs += [pl.BlockSpec(memory_space=pl.ANY)] * 6
            args += list(outs)
            aliases = {6 + k: k for k in range(6)}
        blk = lambda i, g, off=off: (off + i, g)
        out_specs = [pl.BlockSpec((tm, cg), blk)] * 4 + [
            pl.BlockSpec((tm, LANES), lambda i, g, off=off: (off + i, 0)),
            pl.BlockSpec((LANES, tm), lambda i, g, off=off: (0, off + i))]
        outs = pl.pallas_call(
            functools.partial(_ml_in_kernel, req_len=cfg["n"], aliased=bool(aliases)),
            grid=(cfg["nreq"] * cfg["n"] // tm, ngroups),
            in_specs=in_specs, out_specs=out_specs, out_shape=shapes,
            scratch_shapes=[pltpu.VMEM((tm // cfg["n"], cfg["n"] + 2 * CONV_HALO, cg), F32),
                            pltpu.VMEM((tm, LANES), F32)],
            input_output_aliases=aliases,
            compiler_params=_params(2), name="mlstm_in",
        )(*args)
    return outs


def _scan_kernel(*refs, T, has_state):
    L = ML_CHUNK
    nc = T // L
    if has_state:
        (q_ref, k_ref, v_ref, gcol_ref, grow_ref, c0_ref, n0_ref, m0_ref, _, _,
         h0_ref, h1_ref, c_sc, qk_sc) = refs
    else:
        (q_ref, k_ref, v_ref, gcol_ref, grow_ref,
         h0_ref, h1_ref, cn_ref, nn_ref, mn_ref) = refs
    b = pl.program_id(0)
    hh = pl.program_id(1)
    lane = lax.broadcasted_iota(jnp.int32, (1, LANES), 1)
    ti = lax.broadcasted_iota(jnp.int32, (L, L), 0)
    si = lax.broadcasted_iota(jnp.int32, (L, L), 1)
    if not has_state:
        mn_ref[...] = jnp.zeros(mn_ref.shape, F32)
    qk_single = None
    for d in range(2):
        h_ref = h1_ref if d else h0_ref
        if has_state:
            c_sc[...] = c0_ref[0, 0, d, 0]
            n = n0_ref[0, d, 0]
            m = jnp.full((1, 1), m0_ref[b, d * ML_HEADS + hh], F32)
        else:
            m = jnp.zeros((1, 1), F32)
        order = range(nc - 1, -1, -1) if d else range(nc)
        for step, c in enumerate(order):
            rows = slice(c * L, (c + 1) * L)
            qc, kc, vc = q_ref[rows, :], k_ref[rows, :], v_ref[rows, :]
            if nc == 1:
                if qk_single is None:
                    qk_single = _dot_nt(qc, kc)
                qk = qk_single
            elif d == 0:
                qk = _dot_nt(qc, kc)
                qk_sc[c] = qk
            else:
                qk = qk_sc[c]
            gc = gcol_ref[rows, :]

            def col(j, gc=gc):
                return jnp.sum(jnp.where(lane == 8 * hh + j, gc, 0.0), axis=1, keepdims=True)

            i_col, b_col = col(2 * d), col(2 * d + 1)
            i_row = grow_ref[2 * d:2 * d + 1, rows]
            b_row = grow_ref[2 * d + 1:2 * d + 2, rows]
            mask = (si >= ti) if d else (si <= ti)
            dm = jnp.where(mask, b_col - b_row + i_row, -jnp.inf)
            mt = jnp.maximum(b_col + m, jnp.max(dm, axis=1, keepdims=True))
            s = qk * jnp.exp(dm - mt)
            den = jnp.sum(s, axis=1, keepdims=True)
            num = _dot(s.astype(BF16), vc)
            if has_state:
                w_in = jnp.exp(b_col + m - mt)
                num = num + w_in * _dot_nt(qc, c_sc[...].astype(BF16))
                den = den + w_in * jnp.sum(qc.astype(F32) * n, axis=1, keepdims=True)
            h_ref[rows, :] = num / jnp.maximum(jnp.abs(den), jnp.exp(-mt))
            if has_state and step == nc - 1:
                continue
            edge = 0 if d else L - 1
            b_last = b_row[:, edge:edge + 1]
            m_new = mt[edge:edge + 1, :]
            kw = kc.astype(F32) * jnp.exp(b_last - b_col + i_col - m_new)
            upd = _dot_tn(vc, kw.astype(BF16))
            n_upd = jnp.sum(kw, axis=0, keepdims=True)
            if has_state:
                w_state = jnp.exp(b_last + m - m_new)
                c_sc[...] = w_state * c_sc[...] + upd
                n = w_state * n + n_upd
            else:
                cn_ref[0, 0, d, 0] = upd
                nn_ref[0, d, 0] = n_upd
                mn_ref[0, 0, d:d + 1, :] = jnp.broadcast_to(m_new, (1, LANES))
            m = m_new


def _ml_scan(q, k, v, gcol, grow, state_c, state_n, state_m):
    hshape = jax.ShapeDtypeStruct((N_TOK, ML_INNER), F32)
    T = PROMPT_LEN
    rb = lambda b, hh: (b, hh)
    h0, h1, new_c, new_n, new_m = pl.pallas_call(
        functools.partial(_scan_kernel, T=T, has_state=False),
        grid=(N_PROMPT_REQ, ML_HEADS),
        in_specs=[pl.BlockSpec((T, ML_DH), rb)] * 3 + [
            pl.BlockSpec((T, LANES), lambda b, hh: (b, 0)),
            pl.BlockSpec((8, T), lambda b, hh: (hh, b))],
        out_specs=[pl.BlockSpec((T, ML_DH), rb)] * 2 + [
            pl.BlockSpec((1, 1, 2, 1, ML_DH, ML_DH), lambda b, hh: (b, 0, 0, hh, 0, 0)),
            pl.BlockSpec((1, 2, 1, 1, ML_DH), lambda b, hh: (b, 0, hh, 0, 0)),
            pl.BlockSpec((1, 1, 8, LANES), lambda b, hh: (b, hh, 0, 0))],
        out_shape=[hshape, hshape,
                   jax.ShapeDtypeStruct((N_PROMPT_REQ, 1, 2, ML_HEADS, ML_DH, ML_DH), F32),
                   jax.ShapeDtypeStruct((N_PROMPT_REQ, 2, ML_HEADS, 1, ML_DH), F32),
                   jax.ShapeDtypeStruct((N_PROMPT_REQ, ML_HEADS, 8, LANES), F32)],
        compiler_params=_params(2), name="mlstm_scan_prompt",
    )(q, k, v, gcol, grow)
    T = SAMPLE_LEN
    off = N_PROMPT_TOK // T
    rb = lambda b, hh: (off + b, hh)
    h0, h1 = pl.pallas_call(
        functools.partial(_scan_kernel, T=T, has_state=True),
        grid=(N_SAMPLE_REQ, ML_HEADS),
        in_specs=[pl.BlockSpec((T, ML_DH), rb)] * 3 + [
            pl.BlockSpec((T, LANES), lambda b, hh: (off + b, 0)),
            pl.BlockSpec((8, T), lambda b, hh: (hh, off + b)),
            pl.BlockSpec((1, 1, 2, 1, ML_DH, ML_DH), lambda b, hh: (b, 0, 0, hh, 0, 0)),
            pl.BlockSpec((1, 2, 1, 1, ML_DH), lambda b, hh: (b, 0, hh, 0, 0)),
            pl.BlockSpec(memory_space=pltpu.SMEM),
            pl.BlockSpec(memory_space=pl.ANY), pl.BlockSpec(memory_space=pl.ANY)],
        out_specs=[pl.BlockSpec((T, ML_DH), rb)] * 2,
        out_shape=[hshape, hshape],
        scratch_shapes=[pltpu.VMEM((ML_DH, ML_DH), F32), pltpu.VMEM((T // ML_CHUNK, ML_CHUNK, ML_CHUNK), F32)],
        input_output_aliases={8: 0, 9: 1},
        compiler_params=_params(2), name="mlstm_scan_sample",
    )(q, k, v, gcol, grow, state_c,
      state_n.reshape(N_SAMPLE_REQ, 2, ML_HEADS, 1, ML_DH), state_m.reshape(N_SAMPLE_REQ, 2 * ML_HEADS), h0, h1)
    new_n = new_n.reshape(N_PROMPT_REQ, 1, 2, ML_HEADS, ML_DH)
    new_m = jnp.transpose(new_m[:, :, :2, 0], (0, 2, 1)).reshape(N_PROMPT_REQ, 1, 2, ML_HEADS)
    return h0, h1, new_c, new_n, new_m


def _ml_out_kernel(h_ref, h0_ref, h1_ref, xc_ref, woz_ref, gout_ref, skip_ref, wd_ref, *rest):
    h = h_ref[...]
    acc = None
    for hd in range(ML_HEADS):
        sl = slice(ML_DH * hd, ML_DH * (hd + 1))
        o0 = _sigmoid(_dot(h, woz_ref[0, :, sl]))
        o1 = _sigmoid(_dot(h, woz_ref[1, :, sl]))
        z = _dot(h, woz_ref[2, :, sl])
        hs = o0 * h0_ref[:, sl] + o1 * h1_ref[:, sl]
        hn = hs * lax.rsqrt(jnp.mean(hs * hs, axis=1, keepdims=True) + EPS) * gout_ref[:, sl]
        y = ((hn + skip_ref[:, sl] * xc_ref[:, sl]) * _silu(z)).astype(BF16)
        part = _dot(y, wd_ref[sl, :])
        acc = part if acc is None else acc + part
    _mixer_epilogue(acc, *rest)


def _ml_out(h, h0, h1, xc, woz, gout, skip, wdown, x, mod, g2, wr):
    tm = 256
    e_in, e_out, e_shape = _epilogue_specs(tm)
    row = lambda i: (i, 0)
    const = lambda i: (0, 0)
    return pl.pallas_call(
        _ml_out_kernel,
        grid=(N_TOK // tm,),
        in_specs=[pl.BlockSpec((tm, D), row), pl.BlockSpec((tm, ML_INNER), row),
                  pl.BlockSpec((tm, ML_INNER), row), pl.BlockSpec((tm, ML_INNER), row),
                  pl.BlockSpec((3, D, ML_INNER), lambda i: (0, 0, 0)),
                  pl.BlockSpec((1, ML_INNER), const), pl.BlockSpec((1, ML_INNER), const),
                  pl.BlockSpec((ML_INNER, D), const)] + e_in,
        out_specs=e_out, out_shape=e_shape,
        compiler_params=_params(1), name="mlstm_out",
    )(h, h0, h1, xc, woz, gout.reshape(1, ML_INNER), skip.reshape(1, ML_INNER), wdown,
      x, mod, g2.reshape(1, D), wr)


def _conf_in_kernel(*refs, req_len, aliased):
    h_ref, wa_ref, wg_ref, ba_ref, bg_ref, wdw_ref, bdw_ref = refs[:7]
    u_ref, pad_ref = refs[7 + (1 if aliased else 0):]
    h = h_ref[...]
    a = _dot(h, wa_ref[...]) + ba_ref[...]
    g = _dot(h, wg_ref[...]) + bg_ref[...]
    u = a * _sigmoid(g)
    u_ref[...] = _dwconv(pad_ref, u, wdw_ref[...], req_len, CONV_WIDTH) + bdw_ref[...]


def _conf_in(h, w1, b1, wdw, bdw):
    tm, cg = 1024, 256
    ngroups = D // cg
    out = None
    b1 = b1.reshape(1, 2 * D)
    for cfg in (PROMPT, SAMPLE):
        off = cfg["tok0"] // tm
        in_specs = [pl.BlockSpec((tm, D), lambda i, g, off=off: (off + i, 0)),
                    pl.BlockSpec((D, cg), lambda i, g: (0, g)),
                    pl.BlockSpec((D, cg), lambda i, g: (0, ngroups + g)),
                    pl.BlockSpec((1, cg), lambda i, g: (0, g)),
                    pl.BlockSpec((1, cg), lambda i, g: (0, ngroups + g)),
                    pl.BlockSpec((CONV_WIDTH, cg), lambda i, g: (0, g)),
                    pl.BlockSpec((1, cg), lambda i, g: (0, g))]
        args = [h, w1, w1, b1, b1, wdw, bdw.reshape(1, D)]
        aliases = {}
        if out is not None:
            in_specs.append(pl.BlockSpec(memory_space=pl.ANY))
            args.append(out)
            aliases = {7: 0}
        out = pl.pallas_call(
            functools.partial(_conf_in_kernel, req_len=cfg["n"], aliased=bool(aliases)),
            grid=(cfg["nreq"] * cfg["n"] // tm, ngroups),
            in_specs=in_specs,
            out_specs=pl.BlockSpec((tm, cg), lambda i, g, off=off: (off + i, g)),
            out_shape=jax.ShapeDtypeStruct((N_TOK, D), F32),
            scratch_shapes=[pltpu.VMEM((tm // cfg["n"], cfg["n"] + 2 * CONV_HALO, cg), F32)],
            input_output_aliases=aliases,
            compiler_params=_params(2), name="conformer_in",
        )(*args)
    return out


def _route_kernel(aff_ref, slot_ref, slott_ref, *, nreq, n, cap):
    a = jnp.concatenate([aff_ref[:, n * r:n * (r + 1)] for r in range(nreq)], axis=0)
    bits = pltpu.bitcast(a, jnp.int32)
    capf = float(cap)

    def bisect(i, p):
        cand = p | jnp.left_shift(jnp.int32(1), 30 - i)
        cnt = jnp.sum(jnp.where(bits >= cand, 1.0, 0.0), axis=1, keepdims=True)
        return jnp.where(cnt >= capf, cand, p)

    thr = lax.fori_loop(0, 31, bisect, jnp.zeros((a.shape[0], 1), jnp.int32))
    gt = bits > thr
    eq = bits == thr
    need = capf - jnp.sum(jnp.where(gt, 1.0, 0.0), axis=1, keepdims=True)
    before = jnp.where(lax.broadcasted_iota(jnp.int32, (n, n), 0) < lax.broadcasted_iota(jnp.int32, (n, n), 1),
                       1.0, 0.0).astype(BF16)
    eq_rank = _dot(jnp.where(eq, 1.0, 0.0).astype(BF16), before)
    sel = jnp.logical_or(gt, jnp.logical_and(eq, eq_rank < need))
    pos = _dot(jnp.where(sel, 1.0, 0.0).astype(BF16), before)
    slot = jnp.where(sel, pos, -1.0)
    slot_ref[...] = slot
    slott_ref[...] = slot.T


def _route(aff, cfg):
    nreq, n, cap = cfg["nreq"], cfg["n"], cfg["cap"]
    rows = nreq * N_EXPERTS
    half = cfg["tok0"] // (nreq * n)
    return pl.pallas_call(
        functools.partial(_route_kernel, nreq=nreq, n=n, cap=cap),
        grid=(1,),
        in_specs=[pl.BlockSpec((N_EXPERTS, nreq * n), lambda i: (0, half))],
        out_specs=[pl.BlockSpec((rows, n), lambda i: (0, 0)), pl.BlockSpec((n, rows), lambda i: (0, 0))],
        out_shape=[jax.ShapeDtypeStruct((rows, n), F32), jax.ShapeDtypeStruct((n, rows), F32)],
        compiler_params=_params(1), name="moe_route",
    )(aff)


def _gather_kernel(*refs, cap, aliased):
    h_ref, slot_ref, aff_ref = refs[:3]
    xs_ref, vals_ref = refs[3 + (2 if aliased else 0):]
    slot = slot_ref[...]
    aff = aff_ref[...]
    n = slot.shape[1]
    ci = lax.broadcasted_iota(jnp.int32, (cap, n), 0).astype(F32)
    onehots = []
    for e in range(N_EXPERTS):
        hit = slot[e:e + 1, :] == ci
        onehots.append(jnp.where(hit, 1.0, 0.0).astype(BF16))
        val = jnp.sum(jnp.where(hit, aff[e:e + 1, :], 0.0), axis=1, keepdims=True)
        vals_ref[e] = jnp.broadcast_to(val, (cap, LANES))
    xs = _dot(jnp.concatenate(onehots, axis=0), h_ref[...])
    xs_ref[...] = xs.astype(BF16).reshape(N_EXPERTS, cap, D)


def _gather(h2, aff, slots):
    outs = None
    for cfg, slot in zip((PROMPT, SAMPLE), slots):
        n, cap = cfg["n"], cfg["cap"]
        toff, soff = cfg["tok0"] // n, cfg["slot0"] // cap
        in_specs = [pl.BlockSpec((n, D), lambda b, toff=toff: (toff + b, 0)),
                    pl.BlockSpec((N_EXPERTS, n), lambda b: (b, 0)),
                    pl.BlockSpec((N_EXPERTS, n), lambda b, toff=toff: (0, toff + b))]
        args = [h2, slot, aff]
        aliases = {}
        if outs is not None:
            in_specs += [pl.BlockSpec(memory_space=pl.ANY)] * 2
            args += list(outs)
            aliases = {3: 0, 4: 1}
        outs = pl.pallas_call(
            functools.partial(_gather_kernel, cap=cap, aliased=bool(aliases)),
            grid=(cfg["nreq"],),
            in_specs=in_specs,
            out_specs=[pl.BlockSpec((N_EXPERTS, cap, D), lambda b, soff=soff: (0, soff + b, 0)),
                       pl.BlockSpec((N_EXPERTS, cap, LANES), lambda b, soff=soff: (0, soff + b, 0))],
            out_shape=[jax.ShapeDtypeStruct((N_EXPERTS, ROWS_PER_EXPERT, D), BF16),
                       jax.ShapeDtypeStruct((N_EXPERTS, ROWS_PER_EXPERT, LANES), F32)],
            input_output_aliases=aliases,
            compiler_params=_params(1), name="moe_gather",
        )(*args)
    return outs


def _ffn_kernel(xs_ref, vals_ref, wg_ref, wu_ref, wd_ref, ys_ref, wg_sc, wu_sc, wd_sc):
    @pl.when(pl.program_id(1) == 0)
    def _():
        wg_sc[...] = wg_ref[0].astype(BF16)
        wu_sc[...] = wu_ref[0].astype(BF16)
        wd_sc[...] = wd_ref[0].astype(BF16)

    xs = xs_ref[0]
    act = (_silu(_dot(xs, wg_sc[...])) * _dot(xs, wu_sc[...])).astype(BF16)
    ys = _dot(act, wd_sc[...])
    vals = vals_ref[0]
    ys_ref[0] = jnp.concatenate(
        [ys[:, LANES * j:LANES * (j + 1)] * vals for j in range(D // LANES)], axis=1).astype(BF16)


def _ffn(xs, vals, wg, wu, wd):
    tr = 512
    wspec = pl.BlockSpec((1, D, D), lambda e, c: (e, 0, 0))
    return pl.pallas_call(
        _ffn_kernel,
        grid=(N_EXPERTS, ROWS_PER_EXPERT // tr),
        in_specs=[pl.BlockSpec((1, tr, D), lambda e, c: (e, c, 0)),
                  pl.BlockSpec((1, tr, LANES), lambda e, c: (e, c, 0)),
                  wspec, wspec, wspec],
        out_specs=pl.BlockSpec((1, tr, D), lambda e, c: (e, c, 0)),
        out_shape=jax.ShapeDtypeStruct((N_EXPERTS, ROWS_PER_EXPERT, D), BF16),
        scratch_shapes=[pltpu.VMEM((D, D), BF16)] * 3,
        compiler_params=_params(2), name="moe_ffn",
    )(xs, vals, wg, wu, wd)


def _combine_kernel(*refs, cap, final, aliased):
    slott_ref, ys_ref, x_ref, mod_ref = refs[:4]
    b = pl.program_id(0)
    st = slott_ref[...].astype(BF16)
    rb, width = st.shape[1], N_EXPERTS * cap
    ri = lax.broadcasted_iota(jnp.int32, (rb, width), 0)
    ji = lax.broadcasted_iota(jnp.int32, (rb, width), 1)
    shift = int(math.log2(cap))
    expand = jnp.where(ri == N_EXPERTS * b + (ji >> shift), 1.0, 0.0).astype(BF16)
    slot_of = _dot(st, expand)
    want = (lax.broadcasted_iota(jnp.int32, (1, width), 1) & (cap - 1)).astype(F32)
    scatter = jnp.where(slot_of == want, 1.0, 0.0).astype(BF16)
    y = _dot(scatter, ys_ref[...].reshape(width, D))
    x_new = x_ref[...] + mod_ref[0][5:6] * y
    if final:
        g_ref, y_ref = refs[4], refs[5]
        y_ref[...] = x_new * lax.rsqrt(jnp.mean(x_new * x_new, axis=1, keepdims=True) + EPS) * g_ref[...]
    else:
        modn_ref, g_ref = refs[4], refs[5]
        xo_ref, h_ref = refs[6 + (2 if aliased else 0):]
        xo_ref[...] = x_new
        mn = modn_ref[0]
        h_ref[...] = _modulated_norm(x_new, g_ref[...], mn[0:1], mn[1:2]).astype(BF16)


def _combine(slotts, ys, x, mod, g_next, mod_next=None):
    final = mod_next is None
    outs = None
    results = []
    for cfg, slott in zip((PROMPT, SAMPLE), slotts):
        n, cap, nreq = cfg["n"], cfg["cap"], cfg["nreq"]
        toff, soff = cfg["tok0"] // n, cfg["slot0"] // cap
        cidx = (lambda b: 0) if cfg is PROMPT else (lambda b: 1 + b)
        in_specs = [pl.BlockSpec(slott.shape, lambda b: (0, 0)),
                    pl.BlockSpec((N_EXPERTS, cap, D), lambda b, soff=soff: (0, soff + b, 0)),
                    pl.BlockSpec((n, D), lambda b, toff=toff: (toff + b, 0)),
                    pl.BlockSpec((1, 6, D), lambda b, cidx=cidx: (cidx(b), 0, 0))]
        args = [slott, ys, x, mod]
        aliases = {}
        if final:
            in_specs.append(pl.BlockSpec((1, D), lambda b: (0, 0)))
            args.append(g_next.reshape(1, D))
            out_specs = pl.BlockSpec((n, D), lambda b: (b, 0))
            out_shape = jax.ShapeDtypeStruct((nreq * n, D), F32)
        else:
            in_specs += [pl.BlockSpec((1, 6, D), lambda b, cidx=cidx: (cidx(b), 0, 0)),
                         pl.BlockSpec((1, D), lambda b: (0, 0))]
            args += [mod_next, g_next.reshape(1, D)]
            if outs is not None:
                in_specs += [pl.BlockSpec(memory_space=pl.ANY)] * 2
                args += list(outs)
                aliases = {6: 0, 7: 1}
            out_specs = [pl.BlockSpec((n, D), lambda b, toff=toff: (toff + b, 0))] * 2
            out_shape = [jax.ShapeDtypeStruct((N_TOK, D), F32), jax.ShapeDtypeStruct((N_TOK, D), BF16)]
        outs = pl.pallas_call(
            functools.partial(_combine_kernel, cap=cap, final=final, aliased=bool(aliases)),
            grid=(nreq,),
            in_specs=in_specs, out_specs=out_specs, out_shape=out_shape,
            input_output_aliases=aliases,
            compiler_params=_params(1), name="moe_combine",
        )(*args)
        results.append(outs)
        if final:
            outs = None
    return results if final else outs


def _moe(x, h2, aff, mod, moe_w, g_next, mod_next):
    slot_p, slott_p = _route(aff, PROMPT)
    slot_s, slott_s = _route(aff, SAMPLE)
    xs, vals = _gather(h2, aff, (slot_p, slot_s))
    ys = _ffn(xs, vals, *moe_w)
    return _combine((slott_p, slott_s), ys, x, mod, g_next, mod_next)


def _block_diag_tiles(w):
    g = w.reshape(ML_INNER // 256, 64, 4, 4)
    eye = jnp.eye(64, dtype=w.dtype)
    return jnp.einsum("gnio,nm->gnimo", g, eye).reshape(ML_INNER // 256, 256, 256)


def _gate_weights(w_if, b_if):
    H = ML_HEADS
    cols = jnp.stack([w_if[0][:, :H], w_if[0][:, H:], w_if[1][:, :H], w_if[1][:, H:]], axis=-1)
    cols = jnp.pad(cols, ((0, 0), (0, 0), (0, 4))).reshape(3 * ML_INNER, 8 * H)
    w = jnp.pad(cols, ((0, 0), (0, LANES - 8 * H))).astype(BF16).reshape(3, ML_INNER // 256, 256, LANES)
    b = jnp.stack([b_if[0][:H], b_if[0][H:], b_if[1][:H], b_if[1][H:]], axis=-1)
    b = jnp.pad(jnp.pad(b, ((0, 0), (0, 4))).reshape(1, 8 * H), ((0, 0), (0, LANES - 8 * H)))
    return w, b


def kernel(x_prompt, x_sample, cache_k, cache_v, state_C, state_n, state_m, c, c_ctx, w_ada, b_ada, g_norm1, g_norm2, attn_wq, attn_wk, attn_wv, attn_wo, attn_gq, attn_gk, ml_w_up, ml_conv, ml_wq, ml_wk, ml_wv, ml_w_if, ml_b_if, ml_w_o, ml_g_out, ml_skip, ml_w_down, cv_w_pw1, cv_b_pw1, cv_w_dw, cv_b_dw, cv_g_ln, cv_b_ln, cv_w_pw2, cv_b_pw2, moe_router, moe_w_gate, moe_w_up, moe_w_down, g_final):
    x = jnp.concatenate([x_prompt.reshape(N_PROMPT_TOK, D), x_sample.reshape(-1, D)], axis=0)
    cond = jnp.zeros((16, D), F32).at[0].set(c_ctx).at[1:1 + N_SAMPLE_REQ].set(c)
    mods = _ada_table(cond, w_ada, b_ada).reshape(DEPTH, 16, 6, D)
    cache_k = cache_k.reshape(N_SAMPLE_REQ, -1, PAST_LEN, N_KV * HEAD_DIM)
    cache_v = cache_v.reshape(N_SAMPLE_REQ, -1, PAST_LEN, N_KV * HEAD_DIM)

    h = _prenorm(x, mods[0], g_norm1[0])
    new_k, new_v = [], []
    for layer in range(DEPTH):
        kind, j = layer % 3, layer // 3
        mod = mods[layer]
        wr = jnp.pad(moe_router[layer], ((0, 0), (0, LANES - N_EXPERTS)))
        if kind == 0:
            wqkv = jnp.concatenate([attn_wq[j], attn_wk[j], attn_wv[j]], axis=1).astype(BF16)
            q, k, v, k_tok, v_tok = _attn_qkv(h, wqkv, attn_gq[j], attn_gk[j])
            new_k.append(k_tok[:N_PROMPT_TOK].reshape(N_PROMPT_REQ, PROMPT_LEN, N_KV, HEAD_DIM))
            new_v.append(v_tok[:N_PROMPT_TOK].reshape(N_PROMPT_REQ, PROMPT_LEN, N_KV, HEAD_DIM))
            o = _attention(q, k, v, cache_k, cache_v, j)
            x, h2, aff = _dense_out(o, attn_wo[j].astype(BF16), x, mod, g_norm2[layer], wr)
        elif kind == 1:
            wup = ml_w_up[j].astype(BF16)
            wqkv = jnp.stack([_block_diag_tiles(ml_wq[j]), _block_diag_tiles(ml_wk[j]),
                              _block_diag_tiles(ml_wv[j])], axis=1).astype(BF16)
            wif, bif = _gate_weights(ml_w_if[j], ml_b_if[j])
            xc, q, k, v, gcol, grow = _ml_in(h, wup[:, :ML_INNER], ml_conv[j], wqkv, wif, bif)
            h0, h1, new_c, new_n, new_m = _ml_scan(q, k, v, gcol, grow, state_C, state_n, state_m)
            woz = jnp.stack([ml_w_o[j][0].astype(BF16), ml_w_o[j][1].astype(BF16), wup[:, ML_INNER:]])
            x, h2, aff = _ml_out(h, h0, h1, xc, woz, ml_g_out[j], ml_skip[j], ml_w_down[j].astype(BF16),
                                 x, mod, g_norm2[layer], wr)
        else:
            u = _conf_in(h, cv_w_pw1[j].astype(BF16), cv_b_pw1[j], cv_w_dw[j], cv_b_dw[j])
            x, h2, aff = _dense_out(u, cv_w_pw2[j].astype(BF16), x, mod, g_norm2[layer], wr,
                                    bias=cv_b_pw2[j], ln=(cv_g_ln[j], cv_b_ln[j]))
        moe_w = (moe_w_gate[layer], moe_w_up[layer], moe_w_down[layer])
        if layer + 1 < DEPTH:
            x, h = _moe(x, h2, aff, mod, moe_w, g_norm1[layer + 1], mods[layer + 1])
        else:
            y_prompt, y_sample = _moe(x, h2, aff, mod, moe_w, g_final, None)
    return (y_prompt.reshape(N_PROMPT_REQ, PROMPT_LEN, D), y_sample.reshape(N_SAMPLE_REQ, SAMPLE_LEN, D),
            jnp.stack(new_k, axis=1), jnp.stack(new_v, axis=1), new_c, new_n, new_m)
```

```python
import functools
import math

import jax
import jax.numpy as jnp
from jax import lax
from jax.experimental import pallas as pl
from jax.experimental.pallas import tpu as pltpu

F32 = jnp.float32
BF16 = jnp.bfloat16

D = 1024
N_PROMPT_REQ, PROMPT_LEN = 32, 256
N_SAMPLE_REQ, SAMPLE_LEN = 8, 1024
N_PROMPT_TOK = N_PROMPT_REQ * PROMPT_LEN
N_TOK = N_PROMPT_TOK + N_SAMPLE_REQ * SAMPLE_LEN
DEPTH = 4
GRID_W = 64
N_HEADS, N_KV, HEAD_DIM = 16, 4, 64
PAST_LEN = 512
ROPE_THETA = 10000.0
ML_INNER, ML_HEADS, ML_DH = 2048, 4, 512
ML_CONV, ML_CHUNK = 5, 256
CONV_WIDTH = 31
CONV_HALO = 16
N_EXPERTS = 16
CAP_PROMPT = 2 * PROMPT_LEN // N_EXPERTS
CAP_SAMPLE = 2 * SAMPLE_LEN // N_EXPERTS
ROWS_PER_EXPERT = N_PROMPT_REQ * CAP_PROMPT + N_SAMPLE_REQ * CAP_SAMPLE
EPS = 1e-6
LANES = 128
VMEM_LIMIT = 56 * 1024 * 1024
SUB_ROWS = 256

PROMPT = dict(nreq=N_PROMPT_REQ, n=PROMPT_LEN, cap=CAP_PROMPT, tok0=0, slot0=0)
SAMPLE = dict(nreq=N_SAMPLE_REQ, n=SAMPLE_LEN, cap=CAP_SAMPLE, tok0=N_PROMPT_TOK,
              slot0=N_PROMPT_REQ * CAP_PROMPT)


def _params(n_axes):
    return pltpu.CompilerParams(dimension_semantics=("arbitrary",) * n_axes,
                                vmem_limit_bytes=VMEM_LIMIT)


def _cond_index(i, tm):
    npt = N_PROMPT_TOK // tm
    return jnp.where(i < npt, 0, 1 + (i - npt) // (SAMPLE_LEN // tm))


def _row_slices(total, size):
    return [slice(r, r + size) for r in range(0, total, size)]


def _sigmoid(x):
    return 1.0 / (1.0 + jnp.exp(-x))


def _silu(x):
    return x * _sigmoid(x)


def _dot(a, b):
    return jnp.dot(a, b, preferred_element_type=F32)


def _dot_nt(a, b):
    return lax.dot_general(a, b, (((1,), (1,)), ((), ())), preferred_element_type=F32)


def _dot_tn(a, b):
    return lax.dot_general(a, b, (((0,), (0,)), ((), ())), preferred_element_type=F32)


def _split2(x):
    hi = x.astype(BF16)
    return hi, (x - hi.astype(F32)).astype(BF16)


def _split3(x):
    p1 = x.astype(BF16)
    r1 = x - p1.astype(F32)
    p2 = r1.astype(BF16)
    p3 = (r1 - p2.astype(F32)).astype(BF16)
    return p1, p2, p3


def _modulated_norm(x, g, shift, scale):
    ms = jnp.mean(x * x, axis=1, keepdims=True)
    return x * lax.rsqrt(ms + EPS) * g * (1.0 + scale) + shift


def _ada_kernel(c_ref, w_ref, b_ref, o_ref):
    s = _silu(c_ref[...]).astype(BF16)
    o_ref[0] = _dot(s, w_ref[0].astype(BF16)) + b_ref[0]


def _ada_table(cond, w_ada, b_ada):
    return pl.pallas_call(
        _ada_kernel,
        grid=(DEPTH, 6),
        in_specs=[pl.BlockSpec((16, D), lambda l, j: (0, 0)),
                  pl.BlockSpec((1, D, D), lambda l, j: (l, 0, j)),
                  pl.BlockSpec((1, 1, D), lambda l, j: (l, 0, j))],
        out_specs=pl.BlockSpec((1, 16, D), lambda l, j: (l, 0, j)),
        out_shape=jax.ShapeDtypeStruct((DEPTH, 16, 6 * D), F32),
        compiler_params=_params(2), name="ada_table",
    )(cond, w_ada, b_ada.reshape(DEPTH, 1, 6 * D))


def _prenorm_kernel(xp_ref, xs_ref, mod_ref, g_ref, x_ref, h_ref, *, npt):
    m = mod_ref[0]

    def emit(src_ref):
        x = src_ref[...]
        x_ref[...] = x
        h_ref[...] = _modulated_norm(x, g_ref[...], m[0:1], m[1:2]).astype(BF16)

    @pl.when(pl.program_id(0) < npt)
    def _():
        emit(xp_ref)

    @pl.when(pl.program_id(0) >= npt)
    def _():
        emit(xs_ref)


def _prenorm(x_prompt, x_sample, mod, g):
    tm = 512
    npt = N_PROMPT_TOK // tm
    row = lambda i: (i, 0)
    return pl.pallas_call(
        functools.partial(_prenorm_kernel, npt=npt),
        grid=(N_TOK // tm,),
        in_specs=[pl.BlockSpec((tm, D), lambda i: (jnp.minimum(i, npt - 1), 0)),
                  pl.BlockSpec((tm, D), lambda i: (jnp.maximum(i - npt, 0), 0)),
                  pl.BlockSpec((1, 6, D), lambda i: (_cond_index(i, tm), 0, 0)),
                  pl.BlockSpec((1, D), lambda i: (0, 0))],
        out_specs=[pl.BlockSpec((tm, D), row), pl.BlockSpec((tm, D), row)],
        out_shape=[jax.ShapeDtypeStruct((N_TOK, D), F32), jax.ShapeDtypeStruct((N_TOK, D), BF16)],
        compiler_params=_params(1), name="prenorm",
    )(x_prompt, x_sample, mod, g.reshape(1, D))


def _mixer_epilogue(o, rows, x_ref, mod_ref, g2_ref, wr_ref, xo_ref, h2_ref, aff_ref):
    m = mod_ref[0]
    x_new = x_ref[rows, :] + m[2:3] * o
    xo_ref[rows, :] = x_new
    h2 = _modulated_norm(x_new, g2_ref[...], m[3:4], m[4:5])
    hh = h2.astype(BF16)
    h2_ref[rows, :] = hh
    hl = (h2 - hh.astype(F32)).astype(BF16)
    wr = wr_ref[...]
    wh = wr.astype(BF16)
    wl = (wr - wh.astype(F32)).astype(BF16)
    both = _dot(hh, jnp.concatenate([wh, wl], axis=1))
    logits = both[:, :LANES] + both[:, LANES:] + _dot(hl, wh)
    lt = logits.T[:N_EXPERTS]
    e = jnp.exp(lt - jnp.max(lt, axis=0, keepdims=True))
    aff_ref[:, rows] = e / jnp.sum(e, axis=0, keepdims=True)


def _epilogue_specs(tm):
    in_specs = [pl.BlockSpec((tm, D), lambda i: (i, 0)),
                pl.BlockSpec((1, 6, D), lambda i: (_cond_index(i, tm), 0, 0)),
                pl.BlockSpec((1, D), lambda i: (0, 0)),
                pl.BlockSpec((D, LANES), lambda i: (0, 0))]
    out_specs = [pl.BlockSpec((tm, D), lambda i: (i, 0)),
                 pl.BlockSpec((tm, D), lambda i: (i, 0)),
                 pl.BlockSpec((N_EXPERTS, tm), lambda i: (0, i))]
    out_shape = [jax.ShapeDtypeStruct((N_TOK, D), F32),
                 jax.ShapeDtypeStruct((N_TOK, D), BF16),
                 jax.ShapeDtypeStruct((N_EXPERTS, N_TOK), F32)]
    return in_specs, out_specs, out_shape


def _dense_out_kernel(*refs, layer_norm):
    a_ref, w_ref = refs[:2]
    rest = refs[5:] if layer_norm else refs[2:]
    for rows in _row_slices(a_ref.shape[0], SUB_ROWS):
        if layer_norm:
            b_ref, gl_ref, bl_ref = refs[2:5]
            u = a_ref[rows, :]
            mu = jnp.mean(u, axis=1, keepdims=True)
            uc = u - mu
            y = uc * lax.rsqrt(jnp.mean(uc * uc, axis=1, keepdims=True) + EPS)
            a = _silu(y * gl_ref[...] + bl_ref[...]).astype(BF16)
            o = _dot(a, w_ref[...]) + b_ref[...]
        else:
            o = _dot(a_ref[rows, :], w_ref[...])
        _mixer_epilogue(o, rows, *rest)


def _dense_out(a, w, x, mod, g2, wr, bias=None, ln=None):
    tm = 512
    e_in, e_out, e_shape = _epilogue_specs(tm)
    in_specs = [pl.BlockSpec((tm, D), lambda i: (i, 0)), pl.BlockSpec((D, D), lambda i: (0, 0))]
    args = [a, w]
    if ln is not None:
        in_specs += [pl.BlockSpec((1, D), lambda i: (0, 0))] * 3
        args += [bias.reshape(1, D), ln[0].reshape(1, D), ln[1].reshape(1, D)]
    return pl.pallas_call(
        functools.partial(_dense_out_kernel, layer_norm=ln is not None),
        grid=(N_TOK // tm,),
        in_specs=in_specs + e_in, out_specs=e_out, out_shape=e_shape,
        compiler_params=_params(1), name="dense_out",
    )(*args, x, mod, g2.reshape(1, D), wr)


def _head_rms(x, gain):
    w = x.shape[1]
    head_of_lane = lax.broadcasted_iota(jnp.int32, (w, LANES), 0) // HEAD_DIM
    gather = jnp.where(head_of_lane == lax.broadcasted_iota(jnp.int32, (w, LANES), 1), 1.0, 0.0).astype(BF16)
    head_of_col = lax.broadcasted_iota(jnp.int32, (LANES, w), 1) // HEAD_DIM
    spread = jnp.where(head_of_col == lax.broadcasted_iota(jnp.int32, (LANES, w), 0), 1.0, 0.0).astype(BF16)
    ms = _dot((x * x).astype(BF16), gather) * (1.0 / HEAD_DIM)
    hi, lo = _split2(lax.rsqrt(ms + EPS))
    return x * (_dot(hi, spread) + _dot(lo, spread)) * gain


def _rope(x, c, s):
    first = (lax.broadcasted_iota(jnp.int32, (1, LANES), 1) & 31) < 16
    outs = []
    for j in range(x.shape[1] // LANES):
        sl = slice(LANES * j, LANES * (j + 1))
        seg = x[:, sl]
        partner = jnp.where(first, pltpu.roll(seg, LANES - 16, 1), pltpu.roll(seg, 16, 1))
        outs.append(seg * c[:, sl] + partner * s[:, sl])
    return jnp.concatenate(outs, axis=1)


def _spread_kv(x, ones):
    lane = lax.broadcasted_iota(jnp.int32, (1, LANES), 1)
    low = lane < HEAD_DIM
    outs = []
    for kv in range(N_KV):
        pair = x[:, LANES * (kv // 2):LANES * (kv // 2 + 1)]
        swapped = pltpu.roll(pair, HEAD_DIM, 1)
        first, second = (swapped, pair) if kv % 2 else (pair, swapped)
        first = jnp.where(low, first, 1.0 if ones else 0.0)
        second = jnp.where(low, 1.0 if ones else 0.0, second)
        if ones:
            first = jnp.where(lane > HEAD_DIM, 0.0, first)
            second = jnp.where(jnp.logical_and(lane > 0, low), 0.0, second)
        outs += [first, second]
    return jnp.concatenate(outs, axis=1).astype(BF16)


def _qkv_kernel(h_ref, w_ref, gq_ref, gk_ref, ct_ref, st_ref, q_ref, k_ref, v_ref, kt_ref, vt_ref):
    nq, nk = N_HEADS * HEAD_DIM, N_KV * HEAD_DIM
    for rows in _row_slices(h_ref.shape[0], SUB_ROWS):
        qkv = _dot(h_ref[rows, :], w_ref[...])
        q = _head_rms(qkv[:, :nq], gq_ref[...])
        k = _head_rms(qkv[:, nq:nq + nk], gk_ref[...])
        v = qkv[:, nq + nk:]
        kt_ref[rows, :] = k
        vt_ref[rows, :] = v
        c, s = ct_ref[rows, :], st_ref[rows, :]
        q_ref[rows, :] = (_rope(q, c, s) * (HEAD_DIM ** -0.5 * math.log2(math.e))).astype(BF16)
        k_ref[rows, :] = _spread_kv(_rope(k, c[:, :nk], s[:, :nk]), ones=False)
        v_ref[rows, :] = _spread_kv(v, ones=True)


def _rope_tables(tm):
    n = SAMPLE_LEN
    row = jnp.repeat(jnp.arange(n // GRID_W), GRID_W).astype(F32)
    col = jnp.tile(jnp.arange(GRID_W), n // GRID_W).astype(F32)
    nf = HEAD_DIM // 4
    inv = ROPE_THETA ** (-jnp.arange(nf, dtype=F32) / nf)
    ar, ac = row[:, None] * inv, col[:, None] * inv
    cos = jnp.concatenate([jnp.cos(ar), jnp.cos(ar), jnp.cos(ac), jnp.cos(ac)], axis=1)
    sin = jnp.concatenate([-jnp.sin(ar), jnp.sin(ar), -jnp.sin(ac), jnp.sin(ac)], axis=1)
    cos = jnp.concatenate([jnp.ones((tm, HEAD_DIM), F32), cos], axis=0)
    sin = jnp.concatenate([jnp.zeros((tm, HEAD_DIM), F32), sin], axis=0)
    return jnp.tile(cos, (1, N_HEADS)), jnp.tile(sin, (1, N_HEADS))


def _attn_qkv(h, wqkv, gq, gk):
    tm = 512
    npt = N_PROMPT_TOK // tm
    ct, st = _rope_tables(tm)
    nq, nk = N_HEADS * HEAD_DIM, N_KV * HEAD_DIM

    def tab(i):
        return (jnp.where(i < npt, 0, 1 + (i - npt) % (SAMPLE_LEN // tm)), 0)

    row = lambda i: (i, 0)
    const = lambda i: (0, 0)
    return pl.pallas_call(
        _qkv_kernel,
        grid=(N_TOK // tm,),
        in_specs=[pl.BlockSpec((tm, D), row), pl.BlockSpec((D, nq + 2 * nk), const),
                  pl.BlockSpec((1, nq), const), pl.BlockSpec((1, nk), const),
                  pl.BlockSpec((tm, nq), tab), pl.BlockSpec((tm, nq), tab)],
        out_specs=[pl.BlockSpec((tm, nq), row), pl.BlockSpec((tm, 4 * nk), row), pl.BlockSpec((tm, 4 * nk), row),
                   pl.BlockSpec((tm, nk), row), pl.BlockSpec((tm, nk), row)],
        out_shape=[jax.ShapeDtypeStruct((N_TOK, nq), BF16),
                   jax.ShapeDtypeStruct((N_TOK, 4 * nk), BF16),
                   jax.ShapeDtypeStruct((N_TOK, 4 * nk), BF16),
                   jax.ShapeDtypeStruct((N_TOK, nk), F32),
                   jax.ShapeDtypeStruct((N_TOK, nk), F32)],
        compiler_params=_params(1), name="attn_qkv",
    )(h, wqkv, jnp.tile(gq, N_HEADS).reshape(1, nq), jnp.tile(gk, N_KV).reshape(1, nk), ct, st)


def _attn_kernel(*refs, tq, has_cache):
    if has_cache:
        q_ref, k_ref, v_ref, ck_ref, cv_ref, _, o_ref = refs
    else:
        q_ref, k_ref, v_ref, o_ref = refs
    qb = 256
    low = lax.broadcasted_iota(jnp.int32, (1, LANES), 1) < HEAD_DIM
    if has_cache:
        ck = _spread_kv(ck_ref[0, 0], ones=False)
        cv = _spread_kv(cv_ref[0, 0], ones=True)
    for kv in range(N_KV):
        base = 2 * LANES * kv
        operands = []
        for half in range(2):
            sl = slice(base + LANES * half, base + LANES * (half + 1))
            k, v = k_ref[:, sl], v_ref[:, sl]
            if has_cache:
                k = jnp.concatenate([k, ck[:, sl]], axis=0)
                v = jnp.concatenate([v, cv[:, sl]], axis=0)
            operands.append((k, v))

        def block(i, carry, base=base, operands=operands):
            r0 = pl.multiple_of(i * qb, qb)
            q = jnp.concatenate([q_ref[pl.ds(r0, qb), base:base + LANES],
                                 q_ref[pl.ds(r0, qb), base + LANES:base + 2 * LANES]], axis=0)
            outs = []
            for half, (k, v) in enumerate(operands):
                s = _dot_nt(q, k)
                p = jnp.exp2(s - jnp.max(s, axis=1, keepdims=True))
                pv = _dot(p.astype(BF16), v)
                sum_col = 0 if half else HEAD_DIM
                outs.append(pv / pv[:, sum_col:sum_col + 1])
            o = jnp.where(low, outs[0], outs[1]).astype(BF16)
            o_ref[pl.ds(r0, qb), base:base + LANES] = o[:qb]
            o_ref[pl.ds(r0, qb), base + LANES:base + 2 * LANES] = o[qb:]
            return carry

        lax.fori_loop(0, tq // qb, block, 0, unroll=2 if tq // qb > 1 else 1)


def _attention(q, k, v, cache_k, cache_v, layer_j):
    outs = None
    for cfg, has_cache in ((PROMPT, False), (SAMPLE, True)):
        tq, off = cfg["n"], cfg["tok0"] // cfg["n"]
        spec = pl.BlockSpec((tq, D), lambda b, off=off: (off + b, 0))
        in_specs = [spec, spec, spec]
        args = [q, k, v]
        aliases = {}
        if has_cache:
            cspec = pl.BlockSpec((1, 1, PAST_LEN, N_KV * HEAD_DIM), lambda b: (b, layer_j, 0, 0))
            in_specs += [cspec, cspec, pl.BlockSpec(memory_space=pl.ANY)]
            args += [cache_k, cache_v, outs]
            aliases = {5: 0}
        outs = pl.pallas_call(
            functools.partial(_attn_kernel, tq=tq, has_cache=has_cache),
            grid=(cfg["nreq"],),
            in_specs=in_specs, out_specs=spec,
            out_shape=jax.ShapeDtypeStruct((N_TOK, D), BF16),
            input_output_aliases=aliases,
            compiler_params=_params(1), name="attention",
        )(*args)
    return outs


def _conv_fill(pad_ref, x, req_len):
    zeros = jnp.zeros((CONV_HALO, LANES), F32)
    for s in range(x.shape[0] // req_len):
        for j in range(x.shape[1] // LANES):
            pad_ref[s, j, 0:CONV_HALO, :] = zeros
            pad_ref[s, j, CONV_HALO + req_len:2 * CONV_HALO + req_len, :] = zeros
            pad_ref[s, j, CONV_HALO:CONV_HALO + req_len, :] = x[s * req_len:(s + 1) * req_len,
                                                                LANES * j:LANES * (j + 1)]


def _conv_rows(pad_ref, w, req_len, width, rows):
    s, r0 = divmod(rows.start, req_len)
    n = rows.stop - rows.start
    cols = []
    for j in range(pad_ref.shape[1]):
        acc = None
        for d in range(width):
            tap = pad_ref[s, j, pl.ds(CONV_HALO + r0 + d - width // 2, n), :] * w[d:d + 1, LANES * j:LANES * (j + 1)]
            acc = tap if acc is None else acc + tap
        cols.append(acc)
    return cols[0] if len(cols) == 1 else jnp.concatenate(cols, axis=1)


def _log_sigmoid(x):
    return jnp.minimum(x, 0.0) - jnp.log1p(jnp.exp(-jnp.abs(x)))


def _ml_in_kernel(*refs, req_len, aliased):
    (h_ref, wup_ref, wc_ref, wqkv_ref, wif_ref, bif_ref) = refs[:6]
    outs = refs[6 + (6 if aliased else 0):]
    xc_ref, q_ref, k_ref, v_ref, gcol_ref, grow_ref, pad_ref, acc_ref = outs
    g = pl.program_id(1)
    xm = _dot(h_ref[...], wup_ref[...])
    _conv_fill(pad_ref, xm, req_len)
    parts = []
    for rows in _row_slices(xm.shape[0], SUB_ROWS):
        xc = _silu(_conv_rows(pad_ref, wc_ref[...], req_len, ML_CONV, rows))
        xc_ref[rows, :] = xc
        xcb = xc.astype(BF16)
        q = _dot(xcb, wqkv_ref[0, 0]).astype(BF16)
        k = (_dot(xcb, wqkv_ref[0, 1]) * (ML_DH ** -0.5)).astype(BF16)
        v = _dot(xm[rows].astype(BF16), wqkv_ref[0, 2]).astype(BF16)
        q_ref[rows, :] = q
        k_ref[rows, :] = k
        v_ref[rows, :] = v
        parts.append(_dot(q, wif_ref[0, 0]) + _dot(k, wif_ref[1, 0]) + _dot(v, wif_ref[2, 0]))
    part = jnp.concatenate(parts, axis=0)

    @pl.when(g == 0)
    def _():
        acc_ref[...] = part

    @pl.when(g > 0)
    def _():
        acc_ref[...] += part

    @pl.when(g == pl.num_programs(1) - 1)
    def _():
        gates = acc_ref[...] + bif_ref[...]
        lf = _log_sigmoid(gates)
        L = ML_CHUNK
        ti = lax.broadcasted_iota(jnp.int32, (L, L), 0)
        ui = lax.broadcasted_iota(jnp.int32, (L, L), 1)
        tri_f = jnp.where(ui <= ti, 1.0, 0.0).astype(BF16)
        tri_b = jnp.where(ui >= ti, 1.0, 0.0).astype(BF16)
        kind = lax.broadcasted_iota(jnp.int32, (1, LANES), 1) & 7
        rows = []
        for c in range(gates.shape[0] // L):
            p1, p2, p3 = _split3(lf[c * L:(c + 1) * L])
            bf = _dot(tri_f, p1) + _dot(tri_f, p2) + _dot(tri_f, p3)
            bb = _dot(tri_b, p1) + _dot(tri_b, p2) + _dot(tri_b, p3)
            rows.append(jnp.where(kind == 1, bf, jnp.where(kind == 3, bb, gates[c * L:(c + 1) * L])))
        out = jnp.concatenate(rows, axis=0)
        gcol_ref[...] = out
        grow_ref[...] = out.T


def _ml_in(h, wup, wconv, wqkv, wif, bif):
    tm, cg = 1024, 256
    ngroups = ML_INNER // cg
    shapes = [jax.ShapeDtypeStruct((N_TOK, ML_INNER), F32),
              jax.ShapeDtypeStruct((N_TOK, ML_INNER), BF16),
              jax.ShapeDtypeStruct((N_TOK, ML_INNER), BF16),
              jax.ShapeDtypeStruct((N_TOK, ML_INNER), BF16),
              jax.ShapeDtypeStruct((N_TOK, LANES), F32),
              jax.ShapeDtypeStruct((LANES, N_TOK), F32)]
    outs = None
    for cfg in (PROMPT, SAMPLE):
        off = cfg["tok0"] // tm
        in_specs = [pl.BlockSpec((tm, D), lambda i, g, off=off: (off + i, 0)),
                    pl.BlockSpec((D, cg), lambda i, g: (0, g)),
                    pl.BlockSpec((ML_CONV, cg), lambda i, g: (0, g)),
                    pl.BlockSpec((1, 3, cg, cg), lambda i, g: (g, 0, 0, 0)),
                    pl.BlockSpec((3, 1, cg, LANES), lambda i, g: (0, g, 0, 0)),
                    pl.BlockSpec((1, LANES), lambda i, g: (0, 0))]
        args = [h, wup, wconv, wqkv, wif, bif]
        aliases = {}
        if outs is not None:
            in_specs += [pl.BlockSpec(memory_space=pl.ANY)] * 6
            args += list(outs)
            aliases = {6 + k: k for k in range(6)}
        blk = lambda i, g, off=off: (off + i, g)
        out_specs = [pl.BlockSpec((tm, cg), blk)] * 4 + [
            pl.BlockSpec((tm, LANES), lambda i, g, off=off: (off + i, 0)),
            pl.BlockSpec((LANES, tm), lambda i, g, off=off: (0, off + i))]
        outs = pl.pallas_call(
            functools.partial(_ml_in_kernel, req_len=cfg["n"], aliased=bool(aliases)),
            grid=(cfg["nreq"] * cfg["n"] // tm, ngroups),
            in_specs=in_specs, out_specs=out_specs, out_shape=shapes,
            scratch_shapes=[pltpu.VMEM((tm // cfg["n"], cg // LANES, cfg["n"] + 2 * CONV_HALO, LANES), F32),
                            pltpu.VMEM((tm, LANES), F32)],
            input_output_aliases=aliases,
            compiler_params=_params(2), name="mlstm_in",
        )(*args)
    return outs


def _scan_kernel(*refs, T, has_state):
    L = ML_CHUNK
    nc = T // L
    if has_state:
        (q_ref, k_ref, v_ref, gcol_ref, grow_ref, c0_ref, n0_ref, m0_ref, _, _,
         h0_ref, h1_ref, c_sc, qk_sc) = refs
    else:
        (q_ref, k_ref, v_ref, gcol_ref, grow_ref,
         h0_ref, h1_ref, cn_ref, nn_ref, mn_ref) = refs
    b = pl.program_id(0)
    hh = pl.program_id(1)
    lane = lax.broadcasted_iota(jnp.int32, (1, LANES), 1)
    ti = lax.broadcasted_iota(jnp.int32, (L, L), 0)
    si = lax.broadcasted_iota(jnp.int32, (L, L), 1)
    if not has_state:
        mn_ref[...] = jnp.zeros(mn_ref.shape, F32)
    qk_single = None
    for d in range(2):
        h_ref = h1_ref if d else h0_ref
        if has_state:
            c_sc[...] = c0_ref[0, 0, d, 0]
            n = n0_ref[0, d, 0]
            m = jnp.full((1, 1), m0_ref[b, d * ML_HEADS + hh], F32)
        else:
            m = jnp.zeros((1, 1), F32)
        order = range(nc - 1, -1, -1) if d else range(nc)
        for step, c in enumerate(order):
            rows = slice(c * L, (c + 1) * L)
            qc, kc, vc = q_ref[rows, :], k_ref[rows, :], v_ref[rows, :]
            if nc == 1:
                if qk_single is None:
                    qk_single = _dot_nt(qc, kc)
                qk = qk_single
            elif d == 0:
                qk = _dot_nt(qc, kc)
                qk_sc[c] = qk
            else:
                qk = qk_sc[c]
            gc = gcol_ref[rows, :]

            def col(j, gc=gc):
                return jnp.sum(jnp.where(lane == 8 * hh + j, gc, 0.0), axis=1, keepdims=True)

            i_col, b_col = col(2 * d), col(2 * d + 1)
            i_row = grow_ref[2 * d:2 * d + 1, rows]
            b_row = grow_ref[2 * d + 1:2 * d + 2, rows]
            mask = (si >= ti) if d else (si <= ti)
            dm = jnp.where(mask, b_col - b_row + i_row, -jnp.inf)
            mt = jnp.maximum(b_col + m, jnp.max(dm, axis=1, keepdims=True))
            s = qk * jnp.exp(dm - mt)
            den = jnp.sum(s, axis=1, keepdims=True)
            num = _dot(s.astype(BF16), vc)
            if has_state:
                w_in = jnp.exp(b_col + m - mt)
                num = num + w_in * _dot_nt(qc, c_sc[...].astype(BF16))
                den = den + w_in * jnp.sum(qc.astype(F32) * n, axis=1, keepdims=True)
            h_ref[rows, :] = num / jnp.maximum(jnp.abs(den), jnp.exp(-mt))
            if has_state and step == nc - 1:
                continue
            edge = 0 if d else L - 1
            b_last = b_row[:, edge:edge + 1]
            m_new = mt[edge:edge + 1, :]
            kw = kc.astype(F32) * jnp.exp(b_last - b_col + i_col - m_new)
            upd = _dot_tn(vc, kw.astype(BF16))
            n_upd = jnp.sum(kw, axis=0, keepdims=True)
            if has_state:
                w_state = jnp.exp(b_last + m - m_new)
                c_sc[...] = w_state * c_sc[...] + upd
                n = w_state * n + n_upd
            else:
                cn_ref[0, 0, d, 0] = upd
                nn_ref[0, d, 0] = n_upd
                mn_ref[0, 0, d:d + 1, :] = jnp.broadcast_to(m_new, (1, LANES))
            m = m_new


def _ml_scan(q, k, v, gcol, grow, state_c, state_n, state_m):
    hshape = jax.ShapeDtypeStruct((N_TOK, ML_INNER), F32)
    T = PROMPT_LEN
    rb = lambda b, hh: (b, hh)
    h0, h1, new_c, new_n, new_m = pl.pallas_call(
        functools.partial(_scan_kernel, T=T, has_state=False),
        grid=(N_PROMPT_REQ, ML_HEADS),
        in_specs=[pl.BlockSpec((T, ML_DH), rb)] * 3 + [
            pl.BlockSpec((T, LANES), lambda b, hh: (b, 0)),
            pl.BlockSpec((8, T), lambda b, hh: (hh, b))],
        out_specs=[pl.BlockSpec((T, ML_DH), rb)] * 2 + [
            pl.BlockSpec((1, 1, 2, 1, ML_DH, ML_DH), lambda b, hh: (b, 0, 0, hh, 0, 0)),
            pl.BlockSpec((1, 2, 1, 1, ML_DH), lambda b, hh: (b, 0, hh, 0, 0)),
            pl.BlockSpec((1, 1, 8, LANES), lambda b, hh: (b, hh, 0, 0))],
        out_shape=[hshape, hshape,
                   jax.ShapeDtypeStruct((N_PROMPT_REQ, 1, 2, ML_HEADS, ML_DH, ML_DH), F32),
                   jax.ShapeDtypeStruct((N_PROMPT_REQ, 2, ML_HEADS, 1, ML_DH), F32),
                   jax.ShapeDtypeStruct((N_PROMPT_REQ, ML_HEADS, 8, LANES), F32)],
        compiler_params=_params(2), name="mlstm_scan_prompt",
    )(q, k, v, gcol, grow)
    T = SAMPLE_LEN
    off = N_PROMPT_TOK // T
    rb = lambda b, hh: (off + b, hh)
    h0, h1 = pl.pallas_call(
        functools.partial(_scan_kernel, T=T, has_state=True),
        grid=(N_SAMPLE_REQ, ML_HEADS),
        in_specs=[pl.BlockSpec((T, ML_DH), rb)] * 3 + [
            pl.BlockSpec((T, LANES), lambda b, hh: (off + b, 0)),
            pl.BlockSpec((8, T), lambda b, hh: (hh, off + b)),
            pl.BlockSpec((1, 1, 2, 1, ML_DH, ML_DH), lambda b, hh: (b, 0, 0, hh, 0, 0)),
            pl.BlockSpec((1, 2, 1, 1, ML_DH), lambda b, hh: (b, 0, hh, 0, 0)),
            pl.BlockSpec(memory_space=pltpu.SMEM),
            pl.BlockSpec(memory_space=pl.ANY), pl.BlockSpec(memory_space=pl.ANY)],
        out_specs=[pl.BlockSpec((T, ML_DH), rb)] * 2,
        out_shape=[hshape, hshape],
        scratch_shapes=[pltpu.VMEM((ML_DH, ML_DH), F32), pltpu.VMEM((T // ML_CHUNK, ML_CHUNK, ML_CHUNK), F32)],
        input_output_aliases={8: 0, 9: 1},
        compiler_params=_params(2), name="mlstm_scan_sample",
    )(q, k, v, gcol, grow, state_c,
      state_n.reshape(N_SAMPLE_REQ, 2, ML_HEADS, 1, ML_DH), state_m.reshape(N_SAMPLE_REQ, 2 * ML_HEADS), h0, h1)
    new_n = new_n.reshape(N_PROMPT_REQ, 1, 2, ML_HEADS, ML_DH)
    new_m = jnp.transpose(new_m[:, :, :2, 0], (0, 2, 1)).reshape(N_PROMPT_REQ, 1, 2, ML_HEADS)
    return h0, h1, new_c, new_n, new_m


def _ml_out_kernel(h_ref, h0_ref, h1_ref, xc_ref, woz_ref, gout_ref, skip_ref, wd_ref, *rest):
    for rows in _row_slices(h_ref.shape[0], SUB_ROWS):
        h = h_ref[rows, :]
        acc = None
        for hd in range(ML_HEADS):
            sl = slice(ML_DH * hd, ML_DH * (hd + 1))
            o0 = _sigmoid(_dot(h, woz_ref[0, :, sl]))
            o1 = _sigmoid(_dot(h, woz_ref[1, :, sl]))
            z = _dot(h, woz_ref[2, :, sl])
            hs = o0 * h0_ref[rows, sl] + o1 * h1_ref[rows, sl]
            hn = hs * lax.rsqrt(jnp.mean(hs * hs, axis=1, keepdims=True) + EPS) * gout_ref[:, sl]
            y = ((hn + skip_ref[:, sl] * xc_ref[rows, sl]) * _silu(z)).astype(BF16)
            part = _dot(y, wd_ref[sl, :])
            acc = part if acc is None else acc + part
        _mixer_epilogue(acc, rows, *rest)


def _ml_out(h, h0, h1, xc, woz, gout, skip, wdown, x, mod, g2, wr):
    tm = 512
    e_in, e_out, e_shape = _epilogue_specs(tm)
    row = lambda i: (i, 0)
    const = lambda i: (0, 0)
    once = pl.Buffered(1)
    return pl.pallas_call(
        _ml_out_kernel,
        grid=(N_TOK // tm,),
        in_specs=[pl.BlockSpec((tm, D), row), pl.BlockSpec((tm, ML_INNER), row),
                  pl.BlockSpec((tm, ML_INNER), row), pl.BlockSpec((tm, ML_INNER), row),
                  pl.BlockSpec((3, D, ML_INNER), lambda i: (0, 0, 0), pipeline_mode=once),
                  pl.BlockSpec((1, ML_INNER), const), pl.BlockSpec((1, ML_INNER), const),
                  pl.BlockSpec((ML_INNER, D), const, pipeline_mode=once)] + e_in,
        out_specs=e_out, out_shape=e_shape,
        compiler_params=_params(1), name="mlstm_out",
    )(h, h0, h1, xc, woz, gout.reshape(1, ML_INNER), skip.reshape(1, ML_INNER), wdown,
      x, mod, g2.reshape(1, D), wr)


def _conf_in_kernel(*refs, req_len, aliased):
    h_ref, wa_ref, wg_ref, ba_ref, bg_ref, wdw_ref, bdw_ref = refs[:7]
    u_ref, pad_ref = refs[7 + (1 if aliased else 0):]
    h = h_ref[...]
    a = _dot(h, wa_ref[...]) + ba_ref[...]
    g = _dot(h, wg_ref[...]) + bg_ref[...]
    _conv_fill(pad_ref, a * _sigmoid(g), req_len)
    for rows in _row_slices(h.shape[0], SUB_ROWS):
        u_ref[rows, :] = _conv_rows(pad_ref, wdw_ref[...], req_len, CONV_WIDTH, rows) + bdw_ref[...]


def _conf_in(h, w1, b1, wdw, bdw):
    tm, cg = 1024, 256
    ngroups = D // cg
    out = None
    b1 = b1.reshape(1, 2 * D)
    for cfg in (PROMPT, SAMPLE):
        off = cfg["tok0"] // tm
        in_specs = [pl.BlockSpec((tm, D), lambda i, g, off=off: (off + i, 0)),
                    pl.BlockSpec((D, cg), lambda i, g: (0, g)),
                    pl.BlockSpec((D, cg), lambda i, g: (0, ngroups + g)),
                    pl.BlockSpec((1, cg), lambda i, g: (0, g)),
                    pl.BlockSpec((1, cg), lambda i, g: (0, ngroups + g)),
                    pl.BlockSpec((CONV_WIDTH, cg), lambda i, g: (0, g)),
                    pl.BlockSpec((1, cg), lambda i, g: (0, g))]
        args = [h, w1, w1, b1, b1, wdw, bdw.reshape(1, D)]
        aliases = {}
        if out is not None:
            in_specs.append(pl.BlockSpec(memory_space=pl.ANY))
            args.append(out)
            aliases = {7: 0}
        out = pl.pallas_call(
            functools.partial(_conf_in_kernel, req_len=cfg["n"], aliased=bool(aliases)),
            grid=(cfg["nreq"] * cfg["n"] // tm, ngroups),
            in_specs=in_specs,
            out_specs=pl.BlockSpec((tm, cg), lambda i, g, off=off: (off + i, g)),
            out_shape=jax.ShapeDtypeStruct((N_TOK, D), F32),
            scratch_shapes=[pltpu.VMEM((tm // cfg["n"], cg // LANES, cfg["n"] + 2 * CONV_HALO, LANES), F32)],
            input_output_aliases=aliases,
            compiler_params=_params(2), name="conformer_in",
        )(*args)
    return out


def _route_kernel(aff_ref, slot_ref, slott_ref, *, nreq, n, cap):
    a = jnp.concatenate([aff_ref[:, n * r:n * (r + 1)] for r in range(nreq)], axis=0)
    capf = float(cap)

    def bisect(i, p):
        cand = p | jnp.left_shift(jnp.int32(1), 30 - i)
        cnt = jnp.sum(jnp.where(a >= pltpu.bitcast(cand, F32), 1.0, 0.0), axis=1, keepdims=True)
        return jnp.where(cnt >= capf, cand, p)

    thr = pltpu.bitcast(lax.fori_loop(0, 31, bisect, jnp.zeros((a.shape[0], 1), jnp.int32)), F32)
    gt = a > thr
    eq = a == thr
    need = capf - jnp.sum(jnp.where(gt, 1.0, 0.0), axis=1, keepdims=True)
    before = jnp.where(lax.broadcasted_iota(jnp.int32, (n, n), 0) < lax.broadcasted_iota(jnp.int32, (n, n), 1),
                       1.0, 0.0).astype(BF16)
    eq_rank = _dot(jnp.where(eq, 1.0, 0.0).astype(BF16), before)
    sel = jnp.logical_or(gt, jnp.logical_and(eq, eq_rank < need))
    pos = _dot(jnp.where(sel, 1.0, 0.0).astype(BF16), before)
    slot = jnp.where(sel, pos, -1.0)
    slot_ref[...] = slot
    slott_ref[...] = slot.T


def _route(aff, cfg):
    nreq, n, cap = cfg["nreq"], cfg["n"], cfg["cap"]
    rows = nreq * N_EXPERTS
    half = cfg["tok0"] // (nreq * n)
    return pl.pallas_call(
        functools.partial(_route_kernel, nreq=nreq, n=n, cap=cap),
        grid=(1,),
        in_specs=[pl.BlockSpec((N_EXPERTS, nreq * n), lambda i: (0, half))],
        out_specs=[pl.BlockSpec((rows, n), lambda i: (0, 0)), pl.BlockSpec((n, rows), lambda i: (0, 0))],
        out_shape=[jax.ShapeDtypeStruct((rows, n), F32), jax.ShapeDtypeStruct((n, rows), F32)],
        compiler_params=_params(1), name="moe_route",
    )(aff)


def _gather_kernel(*refs, cap, aliased):
    h_ref, slot_ref, aff_ref = refs[:3]
    xs_ref, vals_ref = refs[3 + (2 if aliased else 0):]
    slot = slot_ref[...]
    aff = aff_ref[...]
    n = slot.shape[1]
    ci = lax.broadcasted_iota(jnp.int32, (cap, n), 0).astype(F32)
    onehots = []
    for e in range(N_EXPERTS):
        hit = slot[e:e + 1, :] == ci
        onehots.append(jnp.where(hit, 1.0, 0.0).astype(BF16))
        val = jnp.sum(jnp.where(hit, aff[e:e + 1, :], 0.0), axis=1, keepdims=True)
        vals_ref[e] = jnp.broadcast_to(val, (cap, LANES))
    xs = _dot(jnp.concatenate(onehots, axis=0), h_ref[...])
    xs_ref[...] = xs.astype(BF16).reshape(N_EXPERTS, cap, D)


def _gather(h2, aff, slots):
    outs = None
    for cfg, slot in zip((PROMPT, SAMPLE), slots):
        n, cap = cfg["n"], cfg["cap"]
        toff, soff = cfg["tok0"] // n, cfg["slot0"] // cap
        in_specs = [pl.BlockSpec((n, D), lambda b, toff=toff: (toff + b, 0)),
                    pl.BlockSpec((N_EXPERTS, n), lambda b: (b, 0)),
                    pl.BlockSpec((N_EXPERTS, n), lambda b, toff=toff: (0, toff + b))]
        args = [h2, slot, aff]
        aliases = {}
        if outs is not None:
            in_specs += [pl.BlockSpec(memory_space=pl.ANY)] * 2
            args += list(outs)
            aliases = {3: 0, 4: 1}
        outs = pl.pallas_call(
            functools.partial(_gather_kernel, cap=cap, aliased=bool(aliases)),
            grid=(cfg["nreq"],),
            in_specs=in_specs,
            out_specs=[pl.BlockSpec((N_EXPERTS, cap, D), lambda b, soff=soff: (0, soff + b, 0)),
                       pl.BlockSpec((N_EXPERTS, cap, LANES), lambda b, soff=soff: (0, soff + b, 0))],
            out_shape=[jax.ShapeDtypeStruct((N_EXPERTS, ROWS_PER_EXPERT, D), BF16),
                       jax.ShapeDtypeStruct((N_EXPERTS, ROWS_PER_EXPERT, LANES), F32)],
            input_output_aliases=aliases,
            compiler_params=_params(1), name="moe_gather",
        )(*args)
    return outs


def _ffn_kernel(xs_ref, vals_ref, wg_ref, wu_ref, wd_ref, ys_ref):
    wg = wg_ref[0, 0].astype(BF16)
    wu = wu_ref[0, 0].astype(BF16)
    wd = wd_ref[0, 0].astype(BF16)
    for rows in _row_slices(xs_ref.shape[1], 2 * SUB_ROWS):
        xs = xs_ref[0, rows, :]
        act = (_silu(_dot(xs, wg)) * _dot(xs, wu)).astype(BF16)
        ys = _dot(act, wd)
        vals = vals_ref[0, rows, :]
        ys_ref[0, rows, :] = jnp.concatenate(
            [ys[:, LANES * j:LANES * (j + 1)] * vals for j in range(D // LANES)], axis=1).astype(BF16)


def _ffn(xs, vals, wg, wu, wd, layer):
    wspec = pl.BlockSpec((1, 1, D, D), lambda e: (layer, e, 0, 0))
    rows = lambda e: (e, 0, 0)
    return pl.pallas_call(
        _ffn_kernel,
        grid=(N_EXPERTS,),
        in_specs=[pl.BlockSpec((1, ROWS_PER_EXPERT, D), rows),
                  pl.BlockSpec((1, ROWS_PER_EXPERT, LANES), rows),
                  wspec, wspec, wspec],
        out_specs=pl.BlockSpec((1, ROWS_PER_EXPERT, D), rows),
        out_shape=jax.ShapeDtypeStruct((N_EXPERTS, ROWS_PER_EXPERT, D), BF16),
        compiler_params=_params(1), name="moe_ffn",
    )(xs, vals, wg, wu, wd)


def _combine_kernel(*refs, cap, final, aliased):
    slott_ref, ys_ref, x_ref, mod_ref = refs[:4]
    b = pl.program_id(0)
    st = slott_ref[...].astype(BF16)
    rb, width = st.shape[1], N_EXPERTS * cap
    ri = lax.broadcasted_iota(jnp.int32, (rb, width), 0)
    ji = lax.broadcasted_iota(jnp.int32, (rb, width), 1)
    shift = int(math.log2(cap))
    expand = jnp.where(ri == N_EXPERTS * b + (ji >> shift), 1.0, 0.0).astype(BF16)
    slot_of = _dot(st, expand)
    want = (lax.broadcasted_iota(jnp.int32, (1, width), 1) & (cap - 1)).astype(F32)
    scatter = jnp.where(slot_of == want, 1.0, 0.0).astype(BF16)
    y = _dot(scatter, ys_ref[...].reshape(width, D))
    x_new = x_ref[...] + mod_ref[0][5:6] * y
    if final:
        g_ref, y_ref = refs[4], refs[5]
        y_ref[...] = x_new * lax.rsqrt(jnp.mean(x_new * x_new, axis=1, keepdims=True) + EPS) * g_ref[...]
    else:
        modn_ref, g_ref = refs[4], refs[5]
        xo_ref, h_ref = refs[6 + (2 if aliased else 0):]
        xo_ref[...] = x_new
        mn = modn_ref[0]
        h_ref[...] = _modulated_norm(x_new, g_ref[...], mn[0:1], mn[1:2]).astype(BF16)


def _combine(slotts, ys, x, mod, g_next, mod_next=None):
    final = mod_next is None
    outs = None
    results = []
    for cfg, slott in zip((PROMPT, SAMPLE), slotts):
        n, cap, nreq = cfg["n"], cfg["cap"], cfg["nreq"]
        toff, soff = cfg["tok0"] // n, cfg["slot0"] // cap
        cidx = (lambda b: 0) if cfg is PROMPT else (lambda b: 1 + b)
        in_specs = [pl.BlockSpec(slott.shape, lambda b: (0, 0)),
                    pl.BlockSpec((N_EXPERTS, cap, D), lambda b, soff=soff: (0, soff + b, 0)),
                    pl.BlockSpec((n, D), lambda b, toff=toff: (toff + b, 0)),
                    pl.BlockSpec((1, 6, D), lambda b, cidx=cidx: (cidx(b), 0, 0))]
        args = [slott, ys, x, mod]
        aliases = {}
        if final:
            in_specs.append(pl.BlockSpec((1, D), lambda b: (0, 0)))
            args.append(g_next.reshape(1, D))
            out_specs = pl.BlockSpec((n, D), lambda b: (b, 0))
            out_shape = jax.ShapeDtypeStruct((nreq * n, D), F32)
        else:
            in_specs += [pl.BlockSpec((1, 6, D), lambda b, cidx=cidx: (cidx(b), 0, 0)),
                         pl.BlockSpec((1, D), lambda b: (0, 0))]
            args += [mod_next, g_next.reshape(1, D)]
            if outs is not None:
                in_specs += [pl.BlockSpec(memory_space=pl.ANY)] * 2
                args += list(outs)
                aliases = {6: 0, 7: 1}
            out_specs = [pl.BlockSpec((n, D), lambda b, toff=toff: (toff + b, 0))] * 2
            out_shape = [jax.ShapeDtypeStruct((N_TOK, D), F32), jax.ShapeDtypeStruct((N_TOK, D), BF16)]
        outs = pl.pallas_call(
            functools.partial(_combine_kernel, cap=cap, final=final, aliased=bool(aliases)),
            grid=(nreq,),
            in_specs=in_specs, out_specs=out_specs, out_shape=out_shape,
            input_output_aliases=aliases,
            compiler_params=_params(1), name="moe_combine",
        )(*args)
        results.append(outs)
        if final:
            outs = None
    return results if final else outs


def _moe(x, h2, aff, mod, moe_w, layer, g_next, mod_next):
    slot_p, slott_p = _route(aff, PROMPT)
    slot_s, slott_s = _route(aff, SAMPLE)
    xs, vals = _gather(h2, aff, (slot_p, slot_s))
    ys = _ffn(xs, vals, *moe_w, layer)
    return _combine((slott_p, slott_s), ys, x, mod, g_next, mod_next)


def _block_diag_tiles(w):
    g = w.reshape(ML_INNER // 256, 64, 4, 4)
    eye = jnp.eye(64, dtype=w.dtype)
    return jnp.einsum("gnio,nm->gnimo", g, eye).reshape(ML_INNER // 256, 256, 256)


def _gate_weights(w_if, b_if):
    H = ML_HEADS
    cols = jnp.stack([w_if[0][:, :H], w_if[0][:, H:], w_if[1][:, :H], w_if[1][:, H:]], axis=-1)
    cols = jnp.pad(cols, ((0, 0), (0, 0), (0, 4))).reshape(3 * ML_INNER, 8 * H)
    w = jnp.pad(cols, ((0, 0), (0, LANES - 8 * H))).astype(BF16).reshape(3, ML_INNER // 256, 256, LANES)
    b = jnp.stack([b_if[0][:H], b_if[0][H:], b_if[1][:H], b_if[1][H:]], axis=-1)
    b = jnp.pad(jnp.pad(b, ((0, 0), (0, 4))).reshape(1, 8 * H), ((0, 0), (0, LANES - 8 * H)))
    return w, b


def kernel(x_prompt, x_sample, cache_k, cache_v, state_C, state_n, state_m, c, c_ctx, w_ada, b_ada, g_norm1, g_norm2, attn_wq, attn_wk, attn_wv, attn_wo, attn_gq, attn_gk, ml_w_up, ml_conv, ml_wq, ml_wk, ml_wv, ml_w_if, ml_b_if, ml_w_o, ml_g_out, ml_skip, ml_w_down, cv_w_pw1, cv_b_pw1, cv_w_dw, cv_b_dw, cv_g_ln, cv_b_ln, cv_w_pw2, cv_b_pw2, moe_router, moe_w_gate, moe_w_up, moe_w_down, g_final):
    cond = jnp.zeros((16, D), F32).at[0].set(c_ctx).at[1:1 + N_SAMPLE_REQ].set(c)
    mods = _ada_table(cond, w_ada, b_ada).reshape(DEPTH, 16, 6, D)
    cache_k = cache_k.reshape(N_SAMPLE_REQ, -1, PAST_LEN, N_KV * HEAD_DIM)
    cache_v = cache_v.reshape(N_SAMPLE_REQ, -1, PAST_LEN, N_KV * HEAD_DIM)

    x, h = _prenorm(x_prompt.reshape(N_PROMPT_TOK, D), x_sample.reshape(-1, D), mods[0], g_norm1[0])
    new_k, new_v = [], []
    for layer in range(DEPTH):
        kind, j = layer % 3, layer // 3
        mod = mods[layer]
        wr = jnp.pad(moe_router[layer], ((0, 0), (0, LANES - N_EXPERTS)))
        if kind == 0:
            wqkv = jnp.concatenate([attn_wq[j], attn_wk[j], attn_wv[j]], axis=1).astype(BF16)
            q, k, v, k_tok, v_tok = _attn_qkv(h, wqkv, attn_gq[j], attn_gk[j])
            new_k.append(k_tok[:N_PROMPT_TOK].reshape(N_PROMPT_REQ, PROMPT_LEN, N_KV, HEAD_DIM))
            new_v.append(v_tok[:N_PROMPT_TOK].reshape(N_PROMPT_REQ, PROMPT_LEN, N_KV, HEAD_DIM))
            o = _attention(q, k, v, cache_k, cache_v, j)
            x, h2, aff = _dense_out(o, attn_wo[j].astype(BF16), x, mod, g_norm2[layer], wr)
        elif kind == 1:
            wup = ml_w_up[j].astype(BF16)
            wqkv = jnp.stack([_block_diag_tiles(ml_wq[j]), _block_diag_tiles(ml_wk[j]),
                              _block_diag_tiles(ml_wv[j])], axis=1).astype(BF16)
            wif, bif = _gate_weights(ml_w_if[j], ml_b_if[j])
            xc, q, k, v, gcol, grow = _ml_in(h, wup[:, :ML_INNER], ml_conv[j], wqkv, wif, bif)
            h0, h1, new_c, new_n, new_m = _ml_scan(q, k, v, gcol, grow, state_C, state_n, state_m)
            woz = jnp.stack([ml_w_o[j][0].astype(BF16), ml_w_o[j][1].astype(BF16), wup[:, ML_INNER:]])
            x, h2, aff = _ml_out(h, h0, h1, xc, woz, ml_g_out[j], ml_skip[j], ml_w_down[j].astype(BF16),
                                 x, mod, g_norm2[layer], wr)
        else:
            u = _conf_in(h, cv_w_pw1[j].astype(BF16), cv_b_pw1[j], cv_w_dw[j], cv_b_dw[j])
            x, h2, aff = _dense_out(u, cv_w_pw2[j].astype(BF16), x, mod, g_norm2[layer], wr,
                                    bias=cv_b_pw2[j], ln=(cv_g_ln[j], cv_b_ln[j]))
        moe_w = (moe_w_gate, moe_w_up, moe_w_down)
        if layer + 1 < DEPTH:
            x, h = _moe(x, h2, aff, mod, moe_w, layer, g_norm1[layer + 1], mods[layer + 1])
        else:
            y_prompt, y_sample = _moe(x, h2, aff, mod, moe_w, layer, g_final, None)
    return (y_prompt.reshape(N_PROMPT_REQ, PROMPT_LEN, D), y_sample.reshape(N_SAMPLE_REQ, SAMPLE_LEN, D),
            jnp.stack(new_k, axis=1), jnp.stack(new_v, axis=1), new_c, new_n, new_m)
```

```python
import functools
import math

import jax
import jax.numpy as jnp
from jax import lax
from jax.experimental import pallas as pl
from jax.experimental.pallas import tpu as pltpu

F32 = jnp.float32
BF16 = jnp.bfloat16

D = 1024
N_PROMPT_REQ, PROMPT_LEN = 32, 256
N_SAMPLE_REQ, SAMPLE_LEN = 8, 1024
N_PROMPT_TOK = N_PROMPT_REQ * PROMPT_LEN
N_TOK = N_PROMPT_TOK + N_SAMPLE_REQ * SAMPLE_LEN
DEPTH = 4
GRID_W = 64
N_HEADS, N_KV, HEAD_DIM = 16, 4, 64
PAST_LEN = 512
ROPE_THETA = 10000.0
ML_INNER, ML_HEADS, ML_DH = 2048, 4, 512
ML_CONV, ML_CHUNK = 5, 256
ML_TILE = 256
CONV_WIDTH = 31
CONV_HALO = 16
N_EXPERTS = 16
CAP_PROMPT = 2 * PROMPT_LEN // N_EXPERTS
CAP_SAMPLE = 2 * SAMPLE_LEN // N_EXPERTS
ROWS_PER_EXPERT = N_PROMPT_REQ * CAP_PROMPT + N_SAMPLE_REQ * CAP_SAMPLE
EPS = 1e-6
LANES = 128
VMEM_LIMIT = 56 * 1024 * 1024
SUB_ROWS = 256
STEP_TOK = SAMPLE_LEN
PROMPT_STEPS = N_PROMPT_TOK // STEP_TOK
STEP_SLOTS = CAP_SAMPLE

PROMPT = dict(nreq=N_PROMPT_REQ, n=PROMPT_LEN, cap=CAP_PROMPT, tok0=0)
SAMPLE = dict(nreq=N_SAMPLE_REQ, n=SAMPLE_LEN, cap=CAP_SAMPLE, tok0=N_PROMPT_TOK)


def _params(n_axes):
    return pltpu.CompilerParams(dimension_semantics=("arbitrary",) * n_axes,
                                vmem_limit_bytes=VMEM_LIMIT)


def _cond_index(i, tm):
    npt = N_PROMPT_TOK // tm
    return jnp.where(i < npt, 0, 1 + (i - npt) // (SAMPLE_LEN // tm))


def _row_slices(total, size):
    return [slice(r, r + size) for r in range(0, total, size)]


def _sigmoid(x):
    return 1.0 / (1.0 + jnp.exp(-x))


def _silu(x):
    return x * _sigmoid(x)


def _dot(a, b):
    return jnp.dot(a, b, preferred_element_type=F32)


def _dot_nt(a, b):
    return lax.dot_general(a, b, (((1,), (1,)), ((), ())), preferred_element_type=F32)


def _dot_tn(a, b):
    return lax.dot_general(a, b, (((0,), (0,)), ((), ())), preferred_element_type=F32)


def _split2(x):
    hi = x.astype(BF16)
    return hi, (x - hi.astype(F32)).astype(BF16)


def _split3(x):
    p1 = x.astype(BF16)
    r1 = x - p1.astype(F32)
    p2 = r1.astype(BF16)
    p3 = (r1 - p2.astype(F32)).astype(BF16)
    return p1, p2, p3


def _modulated_norm(x, g, shift, scale):
    ms = jnp.mean(x * x, axis=1, keepdims=True)
    return x * lax.rsqrt(ms + EPS) * g * (1.0 + scale) + shift


def _ada_kernel(c_ref, w_ref, b_ref, o_ref):
    s = _silu(c_ref[...]).astype(BF16)
    o_ref[0] = _dot(s, w_ref[0].astype(BF16)) + b_ref[0]


def _ada_table(cond, w_ada, b_ada):
    return pl.pallas_call(
        _ada_kernel,
        grid=(DEPTH, 6),
        in_specs=[pl.BlockSpec((16, D), lambda l, j: (0, 0)),
                  pl.BlockSpec((1, D, D), lambda l, j: (l, 0, j)),
                  pl.BlockSpec((1, 1, D), lambda l, j: (l, 0, j))],
        out_specs=pl.BlockSpec((1, 16, D), lambda l, j: (l, 0, j)),
        out_shape=jax.ShapeDtypeStruct((DEPTH, 16, 6 * D), F32),
        compiler_params=_params(2), name="ada_table",
    )(cond, w_ada, b_ada.reshape(DEPTH, 1, 6 * D))


def _prenorm_kernel(xp_ref, xs_ref, mod_ref, g_ref, x_ref, h_ref, *, npt):
    m = mod_ref[0]

    def emit(src_ref):
        x = src_ref[...]
        x_ref[...] = x
        h_ref[...] = _modulated_norm(x, g_ref[...], m[0:1], m[1:2]).astype(BF16)

    @pl.when(pl.program_id(0) < npt)
    def _():
        emit(xp_ref)

    @pl.when(pl.program_id(0) >= npt)
    def _():
        emit(xs_ref)


def _prenorm(x_prompt, x_sample, mod, g):
    tm = 512
    npt = N_PROMPT_TOK // tm
    row = lambda i: (i, 0)
    return pl.pallas_call(
        functools.partial(_prenorm_kernel, npt=npt),
        grid=(N_TOK // tm,),
        in_specs=[pl.BlockSpec((tm, D), lambda i: (jnp.minimum(i, npt - 1), 0)),
                  pl.BlockSpec((tm, D), lambda i: (jnp.maximum(i - npt, 0), 0)),
                  pl.BlockSpec((1, 6, D), lambda i: (_cond_index(i, tm), 0, 0)),
                  pl.BlockSpec((1, D), lambda i: (0, 0))],
        out_specs=[pl.BlockSpec((tm, D), row), pl.BlockSpec((tm, D), row)],
        out_shape=[jax.ShapeDtypeStruct((N_TOK, D), F32), jax.ShapeDtypeStruct((N_TOK, D), BF16)],
        compiler_params=_params(1), name="prenorm",
    )(x_prompt, x_sample, mod, g.reshape(1, D))


def _mixer_epilogue(o, rows, x_ref, mod_ref, g2_ref, wr_ref, xo_ref, h2_ref, aff_ref):
    m = mod_ref[0]
    x_new = x_ref[rows, :] + m[2:3] * o
    xo_ref[rows, :] = x_new
    h2 = _modulated_norm(x_new, g2_ref[...], m[3:4], m[4:5])
    hh = h2.astype(BF16)
    h2_ref[rows, :] = hh
    hl = (h2 - hh.astype(F32)).astype(BF16)
    wr = wr_ref[...]
    wh = wr.astype(BF16)
    wl = (wr - wh.astype(F32)).astype(BF16)
    both = _dot(hh, jnp.concatenate([wh, wl], axis=1))
    logits = both[:, :LANES] + both[:, LANES:] + _dot(hl, wh)
    lt = logits.T[:N_EXPERTS]
    e = jnp.exp(lt - jnp.max(lt, axis=0, keepdims=True))
    aff_ref[:, rows] = e / jnp.sum(e, axis=0, keepdims=True)


def _epilogue_specs(tm):
    in_specs = [pl.BlockSpec((tm, D), lambda i: (i, 0)),
                pl.BlockSpec((1, 6, D), lambda i: (_cond_index(i, tm), 0, 0)),
                pl.BlockSpec((1, D), lambda i: (0, 0)),
                pl.BlockSpec((D, LANES), lambda i: (0, 0))]
    out_specs = [pl.BlockSpec((tm, D), lambda i: (i, 0)),
                 pl.BlockSpec((tm, D), lambda i: (i, 0)),
                 pl.BlockSpec((N_EXPERTS, tm), lambda i: (0, i))]
    out_shape = [jax.ShapeDtypeStruct((N_TOK, D), F32),
                 jax.ShapeDtypeStruct((N_TOK, D), BF16),
                 jax.ShapeDtypeStruct((N_EXPERTS, N_TOK), F32)]
    return in_specs, out_specs, out_shape


def _dense_out_kernel(*refs, layer_norm):
    a_ref, w_ref = refs[:2]
    rest = refs[5:] if layer_norm else refs[2:]
    for rows in _row_slices(a_ref.shape[0], SUB_ROWS):
        if layer_norm:
            b_ref, gl_ref, bl_ref = refs[2:5]
            u = a_ref[rows, :]
            mu = jnp.mean(u, axis=1, keepdims=True)
            uc = u - mu
            y = uc * lax.rsqrt(jnp.mean(uc * uc, axis=1, keepdims=True) + EPS)
            a = _silu(y * gl_ref[...] + bl_ref[...]).astype(BF16)
            o = _dot(a, w_ref[...]) + b_ref[...]
        else:
            o = _dot(a_ref[rows, :], w_ref[...])
        _mixer_epilogue(o, rows, *rest)


def _dense_out(a, w, x, mod, g2, wr, bias=None, ln=None):
    tm = 512
    e_in, e_out, e_shape = _epilogue_specs(tm)
    in_specs = [pl.BlockSpec((tm, D), lambda i: (i, 0)), pl.BlockSpec((D, D), lambda i: (0, 0))]
    args = [a, w]
    if ln is not None:
        in_specs += [pl.BlockSpec((1, D), lambda i: (0, 0))] * 3
        args += [bias.reshape(1, D), ln[0].reshape(1, D), ln[1].reshape(1, D)]
    return pl.pallas_call(
        functools.partial(_dense_out_kernel, layer_norm=ln is not None),
        grid=(N_TOK // tm,),
        in_specs=in_specs + e_in, out_specs=e_out, out_shape=e_shape,
        compiler_params=_params(1), name="dense_out",
    )(*args, x, mod, g2.reshape(1, D), wr)


def _head_rms(x, gain):
    w = x.shape[1]
    head_of_lane = lax.broadcasted_iota(jnp.int32, (w, LANES), 0) // HEAD_DIM
    gather = jnp.where(head_of_lane == lax.broadcasted_iota(jnp.int32, (w, LANES), 1), 1.0, 0.0).astype(BF16)
    head_of_col = lax.broadcasted_iota(jnp.int32, (LANES, w), 1) // HEAD_DIM
    spread = jnp.where(head_of_col == lax.broadcasted_iota(jnp.int32, (LANES, w), 0), 1.0, 0.0).astype(BF16)
    ms = _dot((x * x).astype(BF16), gather) * (1.0 / HEAD_DIM)
    hi, lo = _split2(lax.rsqrt(ms + EPS))
    return x * (_dot(hi, spread) + _dot(lo, spread)) * gain


def _rope(x, c, s):
    first = (lax.broadcasted_iota(jnp.int32, (1, LANES), 1) & 31) < 16
    outs = []
    for j in range(x.shape[1] // LANES):
        sl = slice(LANES * j, LANES * (j + 1))
        seg = x[:, sl]
        partner = jnp.where(first, pltpu.roll(seg, LANES - 16, 1), pltpu.roll(seg, 16, 1))
        outs.append(seg * c[:, sl] + partner * s[:, sl])
    return jnp.concatenate(outs, axis=1)


def _spread_kv(x, ones):
    lane = lax.broadcasted_iota(jnp.int32, (1, LANES), 1)
    low = lane < HEAD_DIM
    outs = []
    for kv in range(N_KV):
        pair = x[:, LANES * (kv // 2):LANES * (kv // 2 + 1)]
        swapped = pltpu.roll(pair, HEAD_DIM, 1)
        first, second = (swapped, pair) if kv % 2 else (pair, swapped)
        first = jnp.where(low, first, 1.0 if ones else 0.0)
        second = jnp.where(low, 1.0 if ones else 0.0, second)
        if ones:
            first = jnp.where(lane > HEAD_DIM, 0.0, first)
            second = jnp.where(jnp.logical_and(lane > 0, low), 0.0, second)
        outs += [first, second]
    return jnp.concatenate(outs, axis=1).astype(BF16)


def _qkv_kernel(h_ref, w_ref, gq_ref, gk_ref, ct_ref, st_ref, q_ref, k_ref, v_ref, kt_ref, vt_ref):
    nq, nk = N_HEADS * HEAD_DIM, N_KV * HEAD_DIM
    for rows in _row_slices(h_ref.shape[0], SUB_ROWS):
        qkv = _dot(h_ref[rows, :], w_ref[...])
        q = _head_rms(qkv[:, :nq], gq_ref[...])
        k = _head_rms(qkv[:, nq:nq + nk], gk_ref[...])
        v = qkv[:, nq + nk:]
        kt_ref[rows, :] = k
        vt_ref[rows, :] = v
        c, s = ct_ref[rows, :], st_ref[rows, :]
        q_ref[rows, :] = (_rope(q, c, s) * (HEAD_DIM ** -0.5 * math.log2(math.e))).astype(BF16)
        k_ref[rows, :] = _spread_kv(_rope(k, c[:, :nk], s[:, :nk]), ones=False)
        v_ref[rows, :] = _spread_kv(v, ones=True)


def _rope_tables(tm):
    n = SAMPLE_LEN
    row = jnp.repeat(jnp.arange(n // GRID_W), GRID_W).astype(F32)
    col = jnp.tile(jnp.arange(GRID_W), n // GRID_W).astype(F32)
    nf = HEAD_DIM // 4
    inv = ROPE_THETA ** (-jnp.arange(nf, dtype=F32) / nf)
    ar, ac = row[:, None] * inv, col[:, None] * inv
    cos = jnp.concatenate([jnp.cos(ar), jnp.cos(ar), jnp.cos(ac), jnp.cos(ac)], axis=1)
    sin = jnp.concatenate([-jnp.sin(ar), jnp.sin(ar), -jnp.sin(ac), jnp.sin(ac)], axis=1)
    cos = jnp.concatenate([jnp.ones((tm, HEAD_DIM), F32), cos], axis=0)
    sin = jnp.concatenate([jnp.zeros((tm, HEAD_DIM), F32), sin], axis=0)
    return jnp.tile(cos, (1, N_HEADS)), jnp.tile(sin, (1, N_HEADS))


def _attn_qkv(h, wqkv, gq, gk):
    tm = 512
    npt = N_PROMPT_TOK // tm
    ct, st = _rope_tables(tm)
    nq, nk = N_HEADS * HEAD_DIM, N_KV * HEAD_DIM

    def tab(i):
        return (jnp.where(i < npt, 0, 1 + (i - npt) % (SAMPLE_LEN // tm)), 0)

    row = lambda i: (i, 0)
    const = lambda i: (0, 0)
    return pl.pallas_call(
        _qkv_kernel,
        grid=(N_TOK // tm,),
        in_specs=[pl.BlockSpec((tm, D), row), pl.BlockSpec((D, nq + 2 * nk), const),
                  pl.BlockSpec((1, nq), const), pl.BlockSpec((1, nk), const),
                  pl.BlockSpec((tm, nq), tab), pl.BlockSpec((tm, nq), tab)],
        out_specs=[pl.BlockSpec((tm, nq), row), pl.BlockSpec((tm, 4 * nk), row), pl.BlockSpec((tm, 4 * nk), row),
                   pl.BlockSpec((tm, nk), row), pl.BlockSpec((tm, nk), row)],
        out_shape=[jax.ShapeDtypeStruct((N_TOK, nq), BF16),
                   jax.ShapeDtypeStruct((N_TOK, 4 * nk), BF16),
                   jax.ShapeDtypeStruct((N_TOK, 4 * nk), BF16),
                   jax.ShapeDtypeStruct((N_TOK, nk), F32),
                   jax.ShapeDtypeStruct((N_TOK, nk), F32)],
        compiler_params=_params(1), name="attn_qkv",
    )(h, wqkv, jnp.tile(gq, N_HEADS).reshape(1, nq), jnp.tile(gk, N_KV).reshape(1, nk), ct, st)


def _attend(q_ref, k_ref, v_ref, o_ref, cache, row0, n):
    qb = 256
    low = lax.broadcasted_iota(jnp.int32, (1, LANES), 1) < HEAD_DIM
    for kv in range(N_KV):
        base = 2 * LANES * kv
        operands = []
        for half in range(2):
            sl = slice(base + LANES * half, base + LANES * (half + 1))
            k, v = k_ref[row0:row0 + n, sl], v_ref[row0:row0 + n, sl]
            if cache is not None:
                k = jnp.concatenate([k, cache[0][:, sl]], axis=0)
                v = jnp.concatenate([v, cache[1][:, sl]], axis=0)
            operands.append((k, v))

        def block(i, carry, base=base, operands=operands):
            r0 = row0 + i * qb if isinstance(i, int) else pl.multiple_of(row0 + i * qb, qb)
            q = jnp.concatenate([q_ref[pl.ds(r0, qb), base:base + LANES],
                                 q_ref[pl.ds(r0, qb), base + LANES:base + 2 * LANES]], axis=0)
            outs = []
            for half, (k, v) in enumerate(operands):
                s = _dot_nt(q, k)
                p = jnp.exp2(s - jnp.max(s, axis=1, keepdims=True))
                pv = _dot(p.astype(BF16), v)
                sum_col = 0 if half else HEAD_DIM
                outs.append(pv / pv[:, sum_col:sum_col + 1])
            o = jnp.where(low, outs[0], outs[1]).astype(BF16)
            o_ref[pl.ds(r0, qb), base:base + LANES] = o[:qb]
            o_ref[pl.ds(r0, qb), base + LANES:base + 2 * LANES] = o[qb:]
            return carry

        if n == qb:
            block(0, 0)
        else:
            lax.fori_loop(0, n // qb, block, 0, unroll=2)


def _attn_kernel(q_ref, k_ref, v_ref, ck_ref, cv_ref, o_ref):
    step = pl.program_id(0)

    @pl.when(step < PROMPT_STEPS)
    def _():
        for r in range(STEP_TOK // PROMPT_LEN):
            _attend(q_ref, k_ref, v_ref, o_ref, None, r * PROMPT_LEN, PROMPT_LEN)

    @pl.when(step >= PROMPT_STEPS)
    def _():
        cache = (_spread_kv(ck_ref[0, 0], ones=False), _spread_kv(cv_ref[0, 0], ones=True))
        _attend(q_ref, k_ref, v_ref, o_ref, cache, 0, SAMPLE_LEN)


def _attention(q, k, v, cache_k, cache_v, layer_j):
    spec = pl.BlockSpec((STEP_TOK, D), lambda i: (i, 0))
    cspec = pl.BlockSpec((1, 1, PAST_LEN, N_KV * HEAD_DIM),
                         lambda i: (jnp.maximum(i - PROMPT_STEPS, 0), layer_j, 0, 0))
    return pl.pallas_call(
        _attn_kernel,
        grid=(N_TOK // STEP_TOK,),
        in_specs=[spec, spec, spec, cspec, cspec], out_specs=spec,
        out_shape=jax.ShapeDtypeStruct((N_TOK, D), BF16),
        compiler_params=_params(1), name="attention",
    )(q, k, v, cache_k, cache_v)


def _conv_fill(pad_ref, x, is_prompt):
    nslab, nblk = pad_ref.shape[0], pad_ref.shape[1]
    zeros = jnp.zeros((CONV_HALO, LANES), F32)
    top, bot = slice(0, CONV_HALO), slice(CONV_HALO + SUB_ROWS, 2 * CONV_HALO + SUB_ROWS)
    for s in range(nslab):
        for j in range(nblk):
            lanes = slice(LANES * j, LANES * (j + 1))
            pad_ref[s, j, CONV_HALO:CONV_HALO + SUB_ROWS, :] = x[s * SUB_ROWS:(s + 1) * SUB_ROWS, lanes]

    @pl.when(is_prompt)
    def _():
        for s in range(nslab):
            for j in range(nblk):
                pad_ref[s, j, top, :] = zeros
                pad_ref[s, j, bot, :] = zeros

    @pl.when(jnp.logical_not(is_prompt))
    def _():
        for s in range(nslab):
            for j in range(nblk):
                lanes = slice(LANES * j, LANES * (j + 1))
                r0 = s * SUB_ROWS
                pad_ref[s, j, top, :] = x[r0 - CONV_HALO:r0, lanes] if s else zeros
                pad_ref[s, j, bot, :] = (x[r0 + SUB_ROWS:r0 + SUB_ROWS + CONV_HALO, lanes]
                                         if s + 1 < nslab else zeros)


def _conv_rows(pad_ref, w, width, s):
    cols = []
    for j in range(pad_ref.shape[1]):
        acc = None
        for d in range(width):
            tap = pad_ref[s, j, pl.ds(CONV_HALO + d - width // 2, SUB_ROWS), :] * w[d:d + 1, LANES * j:LANES * (j + 1)]
            acc = tap if acc is None else acc + tap
        cols.append(acc)
    return cols[0] if len(cols) == 1 else jnp.concatenate(cols, axis=1)


def _log_sigmoid(x):
    return jnp.minimum(x, 0.0) - jnp.log1p(jnp.exp(-jnp.abs(x)))


def _ml_in_kernel(h_ref, wup_ref, wc_ref, wqkv_ref, wif_ref, bif_ref,
                  xc_ref, q_ref, k_ref, v_ref, gcol_ref, grow_ref, pad_ref, acc_ref):
    g = pl.program_id(1)
    xm = _dot(h_ref[...], wup_ref[...])
    _conv_fill(pad_ref, xm, pl.program_id(0) < PROMPT_STEPS)
    parts = []
    for s, rows in enumerate(_row_slices(xm.shape[0], SUB_ROWS)):
        xc = _silu(_conv_rows(pad_ref, wc_ref[...], ML_CONV, s))
        xc_ref[rows, :] = xc
        xcb = xc.astype(BF16)
        xmb = xm[rows].astype(BF16)
        part = None
        for t in range(wqkv_ref.shape[0]):
            cols = slice(ML_TILE * t, ML_TILE * (t + 1))
            q = _dot(xcb[:, cols], wqkv_ref[t, 0]).astype(BF16)
            k = (_dot(xcb[:, cols], wqkv_ref[t, 1]) * (ML_DH ** -0.5)).astype(BF16)
            v = _dot(xmb[:, cols], wqkv_ref[t, 2]).astype(BF16)
            q_ref[rows, cols] = q
            k_ref[rows, cols] = k
            v_ref[rows, cols] = v
            gates = _dot(q, wif_ref[0, t]) + _dot(k, wif_ref[1, t]) + _dot(v, wif_ref[2, t])
            part = gates if part is None else part + gates
        parts.append(part)
    part = jnp.concatenate(parts, axis=0)

    @pl.when(g == 0)
    def _():
        acc_ref[...] = part

    @pl.when(g > 0)
    def _():
        acc_ref[...] += part

    @pl.when(g == pl.num_programs(1) - 1)
    def _():
        gates = acc_ref[...] + bif_ref[...]
        lf = _log_sigmoid(gates)
        L = ML_CHUNK
        ti = lax.broadcasted_iota(jnp.int32, (L, L), 0)
        ui = lax.broadcasted_iota(jnp.int32, (L, L), 1)
        tri_f = jnp.where(ui <= ti, 1.0, 0.0).astype(BF16)
        tri_b = jnp.where(ui >= ti, 1.0, 0.0).astype(BF16)
        kind = lax.broadcasted_iota(jnp.int32, (1, LANES), 1) & 7
        rows = []
        for c in range(gates.shape[0] // L):
            p1, p2, p3 = _split3(lf[c * L:(c + 1) * L])
            bf = _dot(tri_f, p1) + _dot(tri_f, p2) + _dot(tri_f, p3)
            bb = _dot(tri_b, p1) + _dot(tri_b, p2) + _dot(tri_b, p3)
            rows.append(jnp.where(kind == 1, bf, jnp.where(kind == 3, bb, gates[c * L:(c + 1) * L])))
        out = jnp.concatenate(rows, axis=0)
        gcol_ref[...] = out
        grow_ref[...] = out.T


def _ml_in(h, wup, wconv, wqkv, wif, bif):
    tm, cg = STEP_TOK, 512
    tiles = cg // ML_TILE
    blk = lambda i, g: (i, g)
    return pl.pallas_call(
        _ml_in_kernel,
        grid=(N_TOK // tm, ML_INNER // cg),
        in_specs=[pl.BlockSpec((tm, D), lambda i, g: (i, 0)),
                  pl.BlockSpec((D, cg), lambda i, g: (0, g)),
                  pl.BlockSpec((ML_CONV, cg), lambda i, g: (0, g)),
                  pl.BlockSpec((tiles, 3, ML_TILE, ML_TILE), lambda i, g: (g, 0, 0, 0)),
                  pl.BlockSpec((3, tiles, ML_TILE, LANES), lambda i, g: (0, g, 0, 0)),
                  pl.BlockSpec((1, LANES), lambda i, g: (0, 0))],
        out_specs=[pl.BlockSpec((tm, cg), blk)] * 4 + [
            pl.BlockSpec((tm, LANES), lambda i, g: (i, 0)),
            pl.BlockSpec((LANES, tm), lambda i, g: (0, i))],
        out_shape=[jax.ShapeDtypeStruct((N_TOK, ML_INNER), F32),
                   jax.ShapeDtypeStruct((N_TOK, ML_INNER), BF16),
                   jax.ShapeDtypeStruct((N_TOK, ML_INNER), BF16),
                   jax.ShapeDtypeStruct((N_TOK, ML_INNER), BF16),
                   jax.ShapeDtypeStruct((N_TOK, LANES), F32),
                   jax.ShapeDtypeStruct((LANES, N_TOK), F32)],
        scratch_shapes=[pltpu.VMEM((tm // SUB_ROWS, cg // LANES, SUB_ROWS + 2 * CONV_HALO, LANES), F32),
                        pltpu.VMEM((tm, LANES), F32)],
        compiler_params=_params(2), name="mlstm_in",
    )(h, wup, wconv, wqkv, wif, bif)


def _scan_request(q_ref, k_ref, v_ref, gcol_ref, grow_ref, h_refs, hh, row0, n, carried, emit_state):
    L = ML_CHUNK
    nc = n // L
    lane = lax.broadcasted_iota(jnp.int32, (1, LANES), 1)
    ti = lax.broadcasted_iota(jnp.int32, (L, L), 0)
    si = lax.broadcasted_iota(jnp.int32, (L, L), 1)
    qk_single = None
    for d in range(2):
        if carried is not None:
            c0_ref, n0_ref, m0, c_sc, qk_sc = carried
            c_sc[...] = c0_ref[0, 0, d, 0]
            nvec = n0_ref[0, d, 0]
            m = jnp.full((1, 1), m0[d], F32)
        else:
            m = jnp.zeros((1, 1), F32)
        order = range(nc - 1, -1, -1) if d else range(nc)
        for step, c in enumerate(order):
            rows = slice(row0 + c * L, row0 + (c + 1) * L)
            qc, kc, vc = q_ref[rows, :], k_ref[rows, :], v_ref[rows, :]
            if nc == 1:
                if qk_single is None:
                    qk_single = _dot_nt(qc, kc)
                qk = qk_single
            elif d == 0:
                qk = _dot_nt(qc, kc)
                qk_sc[c] = qk
            else:
                qk = qk_sc[c]
            gc = gcol_ref[rows, :]

            def col(j, gc=gc):
                return jnp.sum(jnp.where(lane == 8 * hh + j, gc, 0.0), axis=1, keepdims=True)

            i_col, b_col = col(2 * d), col(2 * d + 1)
            i_row = grow_ref[2 * d:2 * d + 1, rows]
            b_row = grow_ref[2 * d + 1:2 * d + 2, rows]
            mask = (si >= ti) if d else (si <= ti)
            dm = jnp.where(mask, b_col - b_row + i_row, -jnp.inf)
            mt = jnp.maximum(b_col + m, jnp.max(dm, axis=1, keepdims=True))
            s = qk * jnp.exp(dm - mt)
            den = jnp.sum(s, axis=1, keepdims=True)
            num = _dot(s.astype(BF16), vc)
            if carried is not None:
                w_in = jnp.exp(b_col + m - mt)
                num = num + w_in * _dot_nt(qc, c_sc[...].astype(BF16))
                den = den + w_in * jnp.sum(qc.astype(F32) * nvec, axis=1, keepdims=True)
            h_refs[d][rows, :] = num / jnp.maximum(jnp.abs(den), jnp.exp(-mt))
            if carried is not None and step == nc - 1:
                continue
            edge = 0 if d else L - 1
            b_last = b_row[:, edge:edge + 1]
            m_new = mt[edge:edge + 1, :]
            kw = kc.astype(F32) * jnp.exp(b_last - b_col + i_col - m_new)
            upd = _dot_tn(vc, kw.astype(BF16))
            n_upd = jnp.sum(kw, axis=0, keepdims=True)
            if carried is not None:
                w_state = jnp.exp(b_last + m - m_new)
                c_sc[...] = w_state * c_sc[...] + upd
                nvec = w_state * nvec + n_upd
            else:
                emit_state(d, upd, n_upd, m_new)
            m = m_new


def _scan_kernel(q_ref, k_ref, v_ref, gcol_ref, grow_ref, c0_ref, n0_ref, m0_ref,
                 h0_ref, h1_ref, cn_ref, nn_ref, mn_ref, c_sc, qk_sc):
    hh = pl.program_id(0)
    step = pl.program_id(1)
    seqs = (q_ref, k_ref, v_ref, gcol_ref, grow_ref, (h0_ref, h1_ref), hh)

    @pl.when(step < PROMPT_STEPS)
    def _():
        mn_ref[...] = jnp.zeros(mn_ref.shape, F32)
        for r in range(STEP_TOK // PROMPT_LEN):
            def emit_state(d, c_new, n_new, m_new, r=r):
                cn_ref[r, 0, d, 0] = c_new
                nn_ref[r, d, 0] = n_new
                mn_ref[r, 0, d:d + 1, :] = jnp.broadcast_to(m_new, (1, LANES))

            _scan_request(*seqs, r * PROMPT_LEN, PROMPT_LEN, None, emit_state)

    @pl.when(step >= PROMPT_STEPS)
    def _():
        b = step - PROMPT_STEPS
        m0 = [m0_ref[b, d * ML_HEADS + hh] for d in range(2)]
        _scan_request(*seqs, 0, SAMPLE_LEN, (c0_ref, n0_ref, m0, c_sc, qk_sc), None)


def _ml_scan(q, k, v, gcol, grow, state_c, state_n, state_m):
    T = STEP_TOK
    per = T // PROMPT_LEN
    rb = lambda hh, i: (i, hh)
    sample = lambda i: jnp.maximum(i - PROMPT_STEPS, 0)
    prompt = lambda i: jnp.minimum(i, PROMPT_STEPS - 1)
    hshape = jax.ShapeDtypeStruct((N_TOK, ML_INNER), F32)
    h0, h1, new_c, new_n, new_m = pl.pallas_call(
        _scan_kernel,
        grid=(ML_HEADS, N_TOK // T),
        in_specs=[pl.BlockSpec((T, ML_DH), rb)] * 3 + [
            pl.BlockSpec((T, LANES), lambda hh, i: (i, 0)),
            pl.BlockSpec((8, T), lambda hh, i: (hh, i)),
            pl.BlockSpec((1, 1, 2, 1, ML_DH, ML_DH), lambda hh, i: (sample(i), 0, 0, hh, 0, 0)),
            pl.BlockSpec((1, 2, 1, 1, ML_DH), lambda hh, i: (sample(i), 0, hh, 0, 0)),
            pl.BlockSpec(memory_space=pltpu.SMEM)],
        out_specs=[pl.BlockSpec((T, ML_DH), rb)] * 2 + [
            pl.BlockSpec((per, 1, 2, 1, ML_DH, ML_DH), lambda hh, i: (prompt(i), 0, 0, hh, 0, 0)),
            pl.BlockSpec((per, 2, 1, 1, ML_DH), lambda hh, i: (prompt(i), 0, hh, 0, 0)),
            pl.BlockSpec((per, 1, 8, LANES), lambda hh, i: (prompt(i), hh, 0, 0))],
        out_shape=[hshape, hshape,
                   jax.ShapeDtypeStruct((N_PROMPT_REQ, 1, 2, ML_HEADS, ML_DH, ML_DH), F32),
                   jax.ShapeDtypeStruct((N_PROMPT_REQ, 2, ML_HEADS, 1, ML_DH), F32),
                   jax.ShapeDtypeStruct((N_PROMPT_REQ, ML_HEADS, 8, LANES), F32)],
        scratch_shapes=[pltpu.VMEM((ML_DH, ML_DH), F32),
                        pltpu.VMEM((SAMPLE_LEN // ML_CHUNK, ML_CHUNK, ML_CHUNK), F32)],
        compiler_params=_params(2), name="mlstm_scan",
    )(q, k, v, gcol, grow, state_c,
      state_n.reshape(N_SAMPLE_REQ, 2, ML_HEADS, 1, ML_DH), state_m.reshape(N_SAMPLE_REQ, 2 * ML_HEADS))
    new_n = new_n.reshape(N_PROMPT_REQ, 1, 2, ML_HEADS, ML_DH)
    new_m = jnp.transpose(new_m[:, :, :2, 0], (0, 2, 1)).reshape(N_PROMPT_REQ, 1, 2, ML_HEADS)
    return h0, h1, new_c, new_n, new_m


def _ml_out_kernel(h_ref, h0_ref, h1_ref, xc_ref, woz_ref, gout_ref, skip_ref, wd_ref, *rest):
    for rows in _row_slices(h_ref.shape[0], SUB_ROWS):
        h = h_ref[rows, :]
        acc = None
        for hd in range(ML_HEADS):
            sl = slice(ML_DH * hd, ML_DH * (hd + 1))
            o0 = _sigmoid(_dot(h, woz_ref[0, :, sl]))
            o1 = _sigmoid(_dot(h, woz_ref[1, :, sl]))
            z = _dot(h, woz_ref[2, :, sl])
            hs = o0 * h0_ref[rows, sl] + o1 * h1_ref[rows, sl]
            hn = hs * lax.rsqrt(jnp.mean(hs * hs, axis=1, keepdims=True) + EPS) * gout_ref[:, sl]
            y = ((hn + skip_ref[:, sl] * xc_ref[rows, sl]) * _silu(z)).astype(BF16)
            part = _dot(y, wd_ref[sl, :])
            acc = part if acc is None else acc + part
        _mixer_epilogue(acc, rows, *rest)


def _ml_out(h, h0, h1, xc, woz, gout, skip, wdown, x, mod, g2, wr):
    tm = 512
    e_in, e_out, e_shape = _epilogue_specs(tm)
    row = lambda i: (i, 0)
    const = lambda i: (0, 0)
    once = pl.Buffered(1)
    return pl.pallas_call(
        _ml_out_kernel,
        grid=(N_TOK // tm,),
        in_specs=[pl.BlockSpec((tm, D), row), pl.BlockSpec((tm, ML_INNER), row),
                  pl.BlockSpec((tm, ML_INNER), row), pl.BlockSpec((tm, ML_INNER), row),
                  pl.BlockSpec((3, D, ML_INNER), lambda i: (0, 0, 0), pipeline_mode=once),
                  pl.BlockSpec((1, ML_INNER), const), pl.BlockSpec((1, ML_INNER), const),
                  pl.BlockSpec((ML_INNER, D), const, pipeline_mode=once)] + e_in,
        out_specs=e_out, out_shape=e_shape,
        compiler_params=_params(1), name="mlstm_out",
    )(h, h0, h1, xc, woz, gout.reshape(1, ML_INNER), skip.reshape(1, ML_INNER), wdown,
      x, mod, g2.reshape(1, D), wr)


def _conf_in_kernel(h_ref, wa_ref, wg_ref, ba_ref, bg_ref, wdw_ref, bdw_ref, u_ref, pad_ref):
    h = h_ref[...]
    a = _dot(h, wa_ref[...]) + ba_ref[...]
    g = _dot(h, wg_ref[...]) + bg_ref[...]
    _conv_fill(pad_ref, a * _sigmoid(g), pl.program_id(0) < PROMPT_STEPS)
    for s, rows in enumerate(_row_slices(h.shape[0], SUB_ROWS)):
        u_ref[rows, :] = _conv_rows(pad_ref, wdw_ref[...], CONV_WIDTH, s) + bdw_ref[...]


def _conf_in(h, w1, b1, wdw, bdw):
    tm, cg = STEP_TOK, 256
    ngroups = D // cg
    b1 = b1.reshape(1, 2 * D)
    return pl.pallas_call(
        _conf_in_kernel,
        grid=(N_TOK // tm, ngroups),
        in_specs=[pl.BlockSpec((tm, D), lambda i, g: (i, 0)),
                  pl.BlockSpec((D, cg), lambda i, g: (0, g)),
                  pl.BlockSpec((D, cg), lambda i, g: (0, ngroups + g)),
                  pl.BlockSpec((1, cg), lambda i, g: (0, g)),
                  pl.BlockSpec((1, cg), lambda i, g: (0, ngroups + g)),
                  pl.BlockSpec((CONV_WIDTH, cg), lambda i, g: (0, g)),
                  pl.BlockSpec((1, cg), lambda i, g: (0, g))],
        out_specs=pl.BlockSpec((tm, cg), lambda i, g: (i, g)),
        out_shape=jax.ShapeDtypeStruct((N_TOK, D), F32),
        scratch_shapes=[pltpu.VMEM((tm // SUB_ROWS, cg // LANES, SUB_ROWS + 2 * CONV_HALO, LANES), F32)],
        compiler_params=_params(2), name="conformer_in",
    )(h, w1, w1, b1, b1, wdw, bdw.reshape(1, D))


def _route_kernel(aff_ref, slot_ref, slott_ref, *, nreq, n, cap):
    a = jnp.concatenate([aff_ref[:, n * r:n * (r + 1)] for r in range(nreq)], axis=0)
    capf = float(cap)

    def bisect(i, p):
        cand = p | jnp.left_shift(jnp.int32(1), 30 - i)
        cnt = jnp.sum(jnp.where(a >= pltpu.bitcast(cand, F32), 1.0, 0.0), axis=1, keepdims=True)
        return jnp.where(cnt >= capf, cand, p)

    thr = pltpu.bitcast(lax.fori_loop(0, 31, bisect, jnp.zeros((a.shape[0], 1), jnp.int32)), F32)
    gt = a > thr
    eq = a == thr
    need = capf - jnp.sum(jnp.where(gt, 1.0, 0.0), axis=1, keepdims=True)
    before = jnp.where(lax.broadcasted_iota(jnp.int32, (n, n), 0) < lax.broadcasted_iota(jnp.int32, (n, n), 1),
                       1.0, 0.0).astype(BF16)
    eq_rank = _dot(jnp.where(eq, 1.0, 0.0).astype(BF16), before)
    sel = jnp.logical_or(gt, jnp.logical_and(eq, eq_rank < need))
    pos = _dot(jnp.where(sel, 1.0, 0.0).astype(BF16), before)
    slot = jnp.where(sel, pos, -1.0)
    slot_ref[...] = slot
    slott_ref[...] = slot.T


def _route(aff, cfg):
    nreq, n, cap = cfg["nreq"], cfg["n"], cfg["cap"]
    rows = nreq * N_EXPERTS
    half = cfg["tok0"] // (nreq * n)
    return pl.pallas_call(
        functools.partial(_route_kernel, nreq=nreq, n=n, cap=cap),
        grid=(1,),
        in_specs=[pl.BlockSpec((N_EXPERTS, nreq * n), lambda i: (0, half))],
        out_specs=[pl.BlockSpec((rows, n), lambda i: (0, 0)), pl.BlockSpec((n, rows), lambda i: (0, 0))],
        out_shape=[jax.ShapeDtypeStruct((rows, n), F32), jax.ShapeDtypeStruct((n, rows), F32)],
        compiler_params=_params(1), name="moe_route",
    )(aff)


def _gather_rows(h, slot, aff, cap):
    n = slot.shape[1]
    ci = lax.broadcasted_iota(jnp.int32, (cap, n), 0).astype(F32)
    onehots, vals = [], []
    for e in range(N_EXPERTS):
        hit = slot[e:e + 1, :] == ci
        onehots.append(jnp.where(hit, 1.0, 0.0).astype(BF16))
        val = jnp.sum(jnp.where(hit, aff[e:e + 1, :], 0.0), axis=1, keepdims=True)
        vals.append(jnp.broadcast_to(val, (cap, LANES)))
    return _dot(jnp.concatenate(onehots, axis=0), h).astype(BF16), vals


def _gather_kernel(h_ref, slotp_ref, slots_ref, aff_ref, xs_ref, vals_ref):
    step = pl.program_id(0)

    @pl.when(step < PROMPT_STEPS)
    def _():
        n, cap = PROMPT_LEN, CAP_PROMPT
        for r in range(STEP_TOK // n):
            xs, vals = _gather_rows(h_ref[n * r:n * (r + 1), :], slotp_ref[N_EXPERTS * r:N_EXPERTS * (r + 1), :],
                                    aff_ref[:, n * r:n * (r + 1)], cap)
            xs_ref[:, cap * r:cap * (r + 1), :] = xs.reshape(N_EXPERTS, cap, D)
            for e in range(N_EXPERTS):
                vals_ref[e, cap * r:cap * (r + 1), :] = vals[e]

    @pl.when(step >= PROMPT_STEPS)
    def _():
        xs, vals = _gather_rows(h_ref[...], slots_ref[...], aff_ref[...], CAP_SAMPLE)
        xs_ref[...] = xs.reshape(N_EXPERTS, CAP_SAMPLE, D)
        for e in range(N_EXPERTS):
            vals_ref[e] = vals[e]


def _gather(h2, aff, slots):
    slot_p, slot_s = slots
    per = STEP_TOK // PROMPT_LEN
    return pl.pallas_call(
        _gather_kernel,
        grid=(N_TOK // STEP_TOK,),
        in_specs=[pl.BlockSpec((STEP_TOK, D), lambda i: (i, 0)),
                  pl.BlockSpec((per * N_EXPERTS, PROMPT_LEN), lambda i: (jnp.minimum(i, PROMPT_STEPS - 1), 0)),
                  pl.BlockSpec((N_EXPERTS, SAMPLE_LEN), lambda i: (jnp.maximum(i - PROMPT_STEPS, 0), 0)),
                  pl.BlockSpec((N_EXPERTS, STEP_TOK), lambda i: (0, i))],
        out_specs=[pl.BlockSpec((N_EXPERTS, STEP_SLOTS, D), lambda i: (0, i, 0)),
                   pl.BlockSpec((N_EXPERTS, STEP_SLOTS, LANES), lambda i: (0, i, 0))],
        out_shape=[jax.ShapeDtypeStruct((N_EXPERTS, ROWS_PER_EXPERT, D), BF16),
                   jax.ShapeDtypeStruct((N_EXPERTS, ROWS_PER_EXPERT, LANES), F32)],
        compiler_params=_params(1), name="moe_gather",
    )(h2, slot_p, slot_s, aff)


def _ffn_kernel(xs_ref, vals_ref, wg_ref, wu_ref, wd_ref, ys_ref):
    wg = wg_ref[0, 0].astype(BF16)
    wu = wu_ref[0, 0].astype(BF16)
    wd = wd_ref[0, 0].astype(BF16)
    for rows in _row_slices(xs_ref.shape[1], 2 * SUB_ROWS):
        xs = xs_ref[0, rows, :]
        act = (_silu(_dot(xs, wg)) * _dot(xs, wu)).astype(BF16)
        ys = _dot(act, wd)
        vals = vals_ref[0, rows, :]
        ys_ref[0, rows, :] = jnp.concatenate(
            [ys[:, LANES * j:LANES * (j + 1)] * vals for j in range(D // LANES)], axis=1).astype(BF16)


def _ffn(xs, vals, wg, wu, wd, layer):
    wspec = pl.BlockSpec((1, 1, D, D), lambda e: (layer, e, 0, 0))
    rows = lambda e: (e, 0, 0)
    return pl.pallas_call(
        _ffn_kernel,
        grid=(N_EXPERTS,),
        in_specs=[pl.BlockSpec((1, ROWS_PER_EXPERT, D), rows),
                  pl.BlockSpec((1, ROWS_PER_EXPERT, LANES), rows),
                  wspec, wspec, wspec],
        out_specs=pl.BlockSpec((1, ROWS_PER_EXPERT, D), rows),
        out_shape=jax.ShapeDtypeStruct((N_EXPERTS, ROWS_PER_EXPERT, D), BF16),
        compiler_params=_params(1), name="moe_ffn",
    )(xs, vals, wg, wu, wd)


def _scatter_rows(slott, ys, req, cap):
    st = slott.astype(BF16)
    rb, width = st.shape[1], N_EXPERTS * cap
    ri = lax.broadcasted_iota(jnp.int32, (rb, width), 0)
    ji = lax.broadcasted_iota(jnp.int32, (rb, width), 1)
    expand = jnp.where(ri == N_EXPERTS * req + (ji >> int(math.log2(cap))), 1.0, 0.0).astype(BF16)
    want = (lax.broadcasted_iota(jnp.int32, (1, width), 1) & (cap - 1)).astype(F32)
    scatter = jnp.where(_dot(st, expand) == want, 1.0, 0.0).astype(BF16)
    return _dot(scatter, ys)


def _combine_kernel(*refs, final):
    slottp_ref, slotts_ref, ys_ref, x_ref, mod_ref = refs[:5]
    step = pl.program_id(0)
    gate = mod_ref[0][5:6]
    if final:
        g_ref, yp_ref, ysm_ref = refs[5:]
    else:
        modn_ref, g_ref, xo_ref, h_ref = refs[5:]

    def emit(rows, x_new, prompt):
        if final:
            y = x_new * lax.rsqrt(jnp.mean(x_new * x_new, axis=1, keepdims=True) + EPS) * g_ref[...]
            (yp_ref if prompt else ysm_ref)[rows, :] = y
        else:
            xo_ref[rows, :] = x_new
            mn = modn_ref[0]
            h_ref[rows, :] = _modulated_norm(x_new, g_ref[...], mn[0:1], mn[1:2]).astype(BF16)

    @pl.when(step < PROMPT_STEPS)
    def _():
        n, cap = PROMPT_LEN, CAP_PROMPT
        per = STEP_TOK // n
        for r in range(per):
            rows = slice(n * r, n * (r + 1))
            ys = ys_ref[:, cap * r:cap * (r + 1), :].reshape(N_EXPERTS * cap, D)
            y = _scatter_rows(slottp_ref[...], ys, step * per + r, cap)
            emit(rows, x_ref[rows, :] + gate * y, True)

    @pl.when(step >= PROMPT_STEPS)
    def _():
        ys = ys_ref[...].reshape(N_EXPERTS * CAP_SAMPLE, D)
        y = _scatter_rows(slotts_ref[...], ys, step - PROMPT_STEPS, CAP_SAMPLE)
        emit(slice(0, STEP_TOK), x_ref[...] + gate * y, False)


def _combine(slotts, ys, x, mod, g_next, mod_next=None):
    final = mod_next is None
    slott_p, slott_s = slotts
    row = lambda i: (i, 0)
    cond = lambda i: (_cond_index(i, STEP_TOK), 0, 0)
    in_specs = [pl.BlockSpec(slott_p.shape, lambda i: (0, 0)),
                pl.BlockSpec(slott_s.shape, lambda i: (0, 0)),
                pl.BlockSpec((N_EXPERTS, STEP_SLOTS, D), lambda i: (0, i, 0)),
                pl.BlockSpec((STEP_TOK, D), row),
                pl.BlockSpec((1, 6, D), cond)]
    args = [slott_p, slott_s, ys, x, mod]
    if final:
        in_specs.append(pl.BlockSpec((1, D), lambda i: (0, 0)))
        args.append(g_next.reshape(1, D))
        out_specs = [pl.BlockSpec((STEP_TOK, D), lambda i: (jnp.minimum(i, PROMPT_STEPS - 1), 0)),
                     pl.BlockSpec((STEP_TOK, D), lambda i: (jnp.maximum(i - PROMPT_STEPS, 0), 0))]
        out_shape = [jax.ShapeDtypeStruct((N_PROMPT_TOK, D), F32),
                     jax.ShapeDtypeStruct((N_TOK - N_PROMPT_TOK, D), F32)]
    else:
        in_specs += [pl.BlockSpec((1, 6, D), cond), pl.BlockSpec((1, D), lambda i: (0, 0))]
        args += [mod_next, g_next.reshape(1, D)]
        out_specs = [pl.BlockSpec((STEP_TOK, D), row)] * 2
        out_shape = [jax.ShapeDtypeStruct((N_TOK, D), F32), jax.ShapeDtypeStruct((N_TOK, D), BF16)]
    return pl.pallas_call(
        functools.partial(_combine_kernel, final=final),
        grid=(N_TOK // STEP_TOK,),
        in_specs=in_specs, out_specs=out_specs, out_shape=out_shape,
        compiler_params=_params(1), name="moe_combine",
    )(*args)


def _moe(x, h2, aff, mod, moe_w, layer, g_next, mod_next):
    slot_p, slott_p = _route(aff, PROMPT)
    slot_s, slott_s = _route(aff, SAMPLE)
    xs, vals = _gather(h2, aff, (slot_p, slot_s))
    ys = _ffn(xs, vals, *moe_w, layer)
    return _combine((slott_p, slott_s), ys, x, mod, g_next, mod_next)


def _block_diag_tiles(w):
    g = w.reshape(ML_INNER // ML_TILE, ML_TILE // 4, 4, 4)
    eye = jnp.eye(ML_TILE // 4, dtype=w.dtype)
    return jnp.einsum("gnio,nm->gnimo", g, eye).reshape(ML_INNER // ML_TILE, ML_TILE, ML_TILE)


def _gate_weights(w_if, b_if):
    H = ML_HEADS
    cols = jnp.stack([w_if[0][:, :H], w_if[0][:, H:], w_if[1][:, :H], w_if[1][:, H:]], axis=-1)
    cols = jnp.pad(cols, ((0, 0), (0, 0), (0, 4))).reshape(3 * ML_INNER, 8 * H)
    w = jnp.pad(cols, ((0, 0), (0, LANES - 8 * H))).astype(BF16).reshape(3, ML_INNER // ML_TILE, ML_TILE, LANES)
    b = jnp.stack([b_if[0][:H], b_if[0][H:], b_if[1][:H], b_if[1][H:]], axis=-1)
    b = jnp.pad(jnp.pad(b, ((0, 0), (0, 4))).reshape(1, 8 * H), ((0, 0), (0, LANES - 8 * H)))
    return w, b


def kernel(x_prompt, x_sample, cache_k, cache_v, state_C, state_n, state_m, c, c_ctx, w_ada, b_ada, g_norm1, g_norm2, attn_wq, attn_wk, attn_wv, attn_wo, attn_gq, attn_gk, ml_w_up, ml_conv, ml_wq, ml_wk, ml_wv, ml_w_if, ml_b_if, ml_w_o, ml_g_out, ml_skip, ml_w_down, cv_w_pw1, cv_b_pw1, cv_w_dw, cv_b_dw, cv_g_ln, cv_b_ln, cv_w_pw2, cv_b_pw2, moe_router, moe_w_gate, moe_w_up, moe_w_down, g_final):
    cond = jnp.zeros((16, D), F32).at[0].set(c_ctx).at[1:1 + N_SAMPLE_REQ].set(c)
    mods = _ada_table(cond, w_ada, b_ada).reshape(DEPTH, 16, 6, D)
    cache_k = cache_k.reshape(N_SAMPLE_REQ, -1, PAST_LEN, N_KV * HEAD_DIM)
    cache_v = cache_v.reshape(N_SAMPLE_REQ, -1, PAST_LEN, N_KV * HEAD_DIM)

    x, h = _prenorm(x_prompt.reshape(N_PROMPT_TOK, D), x_sample.reshape(-1, D), mods[0], g_norm1[0])
    new_k, new_v = [], []
    for layer in range(DEPTH):
        kind, j = layer % 3, layer // 3
        mod = mods[layer]
        wr = jnp.pad(moe_router[layer], ((0, 0), (0, LANES - N_EXPERTS)))
        if kind == 0:
            wqkv = jnp.concatenate([attn_wq[j], attn_wk[j], attn_wv[j]], axis=1).astype(BF16)
            q, k, v, k_tok, v_tok = _attn_qkv(h, wqkv, attn_gq[j], attn_gk[j])
            new_k.append(k_tok[:N_PROMPT_TOK].reshape(N_PROMPT_REQ, PROMPT_LEN, N_KV, HEAD_DIM))
            new_v.append(v_tok[:N_PROMPT_TOK].reshape(N_PROMPT_REQ, PROMPT_LEN, N_KV, HEAD_DIM))
            o = _attention(q, k, v, cache_k, cache_v, j)
            x, h2, aff = _dense_out(o, attn_wo[j].astype(BF16), x, mod, g_norm2[layer], wr)
        elif kind == 1:
            wup = ml_w_up[j].astype(BF16)
            wqkv = jnp.stack([_block_diag_tiles(ml_wq[j]), _block_diag_tiles(ml_wk[j]),
                              _block_diag_tiles(ml_wv[j])], axis=1).astype(BF16)
            wif, bif = _gate_weights(ml_w_if[j], ml_b_if[j])
            xc, q, k, v, gcol, grow = _ml_in(h, wup[:, :ML_INNER], ml_conv[j], wqkv, wif, bif)
            h0, h1, new_c, new_n, new_m = _ml_scan(q, k, v, gcol, grow, state_C, state_n, state_m)
            woz = jnp.stack([ml_w_o[j][0].astype(BF16), ml_w_o[j][1].astype(BF16), wup[:, ML_INNER:]])
            x, h2, aff = _ml_out(h, h0, h1, xc, woz, ml_g_out[j], ml_skip[j], ml_w_down[j].astype(BF16),
                                 x, mod, g_norm2[layer], wr)
        else:
            u = _conf_in(h, cv_w_pw1[j].astype(BF16), cv_b_pw1[j], cv_w_dw[j], cv_b_dw[j])
            x, h2, aff = _dense_out(u, cv_w_pw2[j].astype(BF16), x, mod, g_norm2[layer], wr,
                                    bias=cv_b_pw2[j], ln=(cv_g_ln[j], cv_b_ln[j]))
        moe_w = (moe_w_gate, moe_w_up, moe_w_down)
        if layer + 1 < DEPTH:
            x, h = _moe(x, h2, aff, mod, moe_w, layer, g_norm1[layer + 1], mods[layer + 1])
        else:
            y_prompt, y_sample = _moe(x, h2, aff, mod, moe_w, layer, g_final, None)
    return (y_prompt.reshape(N_PROMPT_REQ, PROMPT_LEN, D), y_sample.reshape(N_SAMPLE_REQ, SAMPLE_LEN, D),
            jnp.stack(new_k, axis=1), jnp.stack(new_v, axis=1), new_c, new_n, new_m)
```

```python
import functools
import math

import jax
import jax.numpy as jnp
from jax import lax
from jax.experimental import pallas as pl
from jax.experimental.pallas import tpu as pltpu

F32 = jnp.float32
BF16 = jnp.bfloat16

D = 1024
N_PROMPT_REQ, PROMPT_LEN = 32, 256
N_SAMPLE_REQ, SAMPLE_LEN = 8, 1024
N_PROMPT_TOK = N_PROMPT_REQ * PROMPT_LEN
N_TOK = N_PROMPT_TOK + N_SAMPLE_REQ * SAMPLE_LEN
DEPTH = 4
GRID_W = 64
N_HEADS, N_KV, HEAD_DIM = 16, 4, 64
PAST_LEN = 512
ROPE_THETA = 10000.0
ML_INNER, ML_HEADS, ML_DH = 2048, 4, 512
ML_CONV, ML_CHUNK = 5, 256
ML_TILE = 256
CONV_WIDTH = 31
CONV_HALO = 16
N_EXPERTS = 16
CAP_PROMPT = 2 * PROMPT_LEN // N_EXPERTS
CAP_SAMPLE = 2 * SAMPLE_LEN // N_EXPERTS
ROWS_PER_EXPERT = N_PROMPT_REQ * CAP_PROMPT + N_SAMPLE_REQ * CAP_SAMPLE
EPS = 1e-6
LANES = 128
VMEM_LIMIT = 56 * 1024 * 1024
SUB_ROWS = 256
STEP_TOK = SAMPLE_LEN
PROMPT_STEPS = N_PROMPT_TOK // STEP_TOK
STEP_SLOTS = CAP_SAMPLE

PROMPT = dict(nreq=N_PROMPT_REQ, n=PROMPT_LEN, cap=CAP_PROMPT, tok0=0)
SAMPLE = dict(nreq=N_SAMPLE_REQ, n=SAMPLE_LEN, cap=CAP_SAMPLE, tok0=N_PROMPT_TOK)


def _params(n_axes):
    return pltpu.CompilerParams(dimension_semantics=("arbitrary",) * n_axes,
                                vmem_limit_bytes=VMEM_LIMIT)


def _cond_index(i, tm):
    npt = N_PROMPT_TOK // tm
    return jnp.where(i < npt, 0, 1 + (i - npt) // (SAMPLE_LEN // tm))


def _row_slices(total, size):
    return [slice(r, r + size) for r in range(0, total, size)]


def _sigmoid(x):
    return 1.0 / (1.0 + jnp.exp(-x))


def _silu(x):
    return x * _sigmoid(x)


def _dot(a, b):
    return jnp.dot(a, b, preferred_element_type=F32)


def _dot_nt(a, b):
    return lax.dot_general(a, b, (((1,), (1,)), ((), ())), preferred_element_type=F32)


def _dot_tn(a, b):
    return lax.dot_general(a, b, (((0,), (0,)), ((), ())), preferred_element_type=F32)


def _split2(x):
    hi = x.astype(BF16)
    return hi, (x - hi.astype(F32)).astype(BF16)


def _split3(x):
    p1 = x.astype(BF16)
    r1 = x - p1.astype(F32)
    p2 = r1.astype(BF16)
    p3 = (r1 - p2.astype(F32)).astype(BF16)
    return p1, p2, p3


def _modulated_norm(x, g, shift, scale):
    ms = jnp.mean(x * x, axis=1, keepdims=True)
    return x * lax.rsqrt(ms + EPS) * g * (1.0 + scale) + shift


def _ada_kernel(c_ref, w_ref, b_ref, o_ref):
    s = _silu(c_ref[...]).astype(BF16)
    o_ref[0] = _dot(s, w_ref[0].astype(BF16)) + b_ref[0]


def _ada_table(cond, w_ada, b_ada):
    return pl.pallas_call(
        _ada_kernel,
        grid=(DEPTH, 6),
        in_specs=[pl.BlockSpec((16, D), lambda l, j: (0, 0)),
                  pl.BlockSpec((1, D, D), lambda l, j: (l, 0, j)),
                  pl.BlockSpec((1, 1, D), lambda l, j: (l, 0, j))],
        out_specs=pl.BlockSpec((1, 16, D), lambda l, j: (l, 0, j)),
        out_shape=jax.ShapeDtypeStruct((DEPTH, 16, 6 * D), F32),
        compiler_params=_params(2), name="ada_table",
    )(cond, w_ada, b_ada.reshape(DEPTH, 1, 6 * D))


def _prenorm_kernel(xp_ref, xs_ref, mod_ref, g_ref, x_ref, h_ref, *, npt):
    m = mod_ref[0]

    def emit(src_ref):
        x = src_ref[...]
        x_ref[...] = x
        h_ref[...] = _modulated_norm(x, g_ref[...], m[0:1], m[1:2]).astype(BF16)

    @pl.when(pl.program_id(0) < npt)
    def _():
        emit(xp_ref)

    @pl.when(pl.program_id(0) >= npt)
    def _():
        emit(xs_ref)


def _prenorm(x_prompt, x_sample, mod, g):
    tm = 512
    npt = N_PROMPT_TOK // tm
    row = lambda i: (i, 0)
    return pl.pallas_call(
        functools.partial(_prenorm_kernel, npt=npt),
        grid=(N_TOK // tm,),
        in_specs=[pl.BlockSpec((tm, D), lambda i: (jnp.minimum(i, npt - 1), 0)),
                  pl.BlockSpec((tm, D), lambda i: (jnp.maximum(i - npt, 0), 0)),
                  pl.BlockSpec((1, 6, D), lambda i: (_cond_index(i, tm), 0, 0)),
                  pl.BlockSpec((1, D), lambda i: (0, 0))],
        out_specs=[pl.BlockSpec((tm, D), row), pl.BlockSpec((tm, D), row)],
        out_shape=[jax.ShapeDtypeStruct((N_TOK, D), F32), jax.ShapeDtypeStruct((N_TOK, D), BF16)],
        compiler_params=_params(1), name="prenorm",
    )(x_prompt, x_sample, mod, g.reshape(1, D))


def _mixer_epilogue(o, rows, x_ref, mod_ref, g2_ref, wr_ref, xo_ref, h2_ref, aff_ref):
    m = mod_ref[0]
    x_new = x_ref[rows, :] + m[2:3] * o
    xo_ref[rows, :] = x_new
    h2 = _modulated_norm(x_new, g2_ref[...], m[3:4], m[4:5])
    hh = h2.astype(BF16)
    h2_ref[rows, :] = hh
    hl = (h2 - hh.astype(F32)).astype(BF16)
    wr = wr_ref[...]
    wh = wr.astype(BF16)
    wl = (wr - wh.astype(F32)).astype(BF16)
    both = _dot(hh, jnp.concatenate([wh, wl], axis=1))
    logits = both[:, :LANES] + both[:, LANES:] + _dot(hl, wh)
    lt = logits.T[:N_EXPERTS]
    e = jnp.exp(lt - jnp.max(lt, axis=0, keepdims=True))
    aff_ref[:, rows] = e / jnp.sum(e, axis=0, keepdims=True)


def _epilogue_specs(tm):
    in_specs = [pl.BlockSpec((tm, D), lambda i: (i, 0)),
                pl.BlockSpec((1, 6, D), lambda i: (_cond_index(i, tm), 0, 0)),
                pl.BlockSpec((1, D), lambda i: (0, 0)),
                pl.BlockSpec((D, LANES), lambda i: (0, 0))]
    out_specs = [pl.BlockSpec((tm, D), lambda i: (i, 0)),
                 pl.BlockSpec((tm, D), lambda i: (i, 0)),
                 pl.BlockSpec((N_EXPERTS, tm), lambda i: (0, i))]
    out_shape = [jax.ShapeDtypeStruct((N_TOK, D), F32),
                 jax.ShapeDtypeStruct((N_TOK, D), BF16),
                 jax.ShapeDtypeStruct((N_EXPERTS, N_TOK), F32)]
    return in_specs, out_specs, out_shape


def _dense_out_kernel(*refs, layer_norm):
    a_ref, w_ref = refs[:2]
    rest = refs[5:] if layer_norm else refs[2:]
    for rows in _row_slices(a_ref.shape[0], SUB_ROWS):
        if layer_norm:
            b_ref, gl_ref, bl_ref = refs[2:5]
            u = a_ref[rows, :]
            mu = jnp.mean(u, axis=1, keepdims=True)
            uc = u - mu
            y = uc * lax.rsqrt(jnp.mean(uc * uc, axis=1, keepdims=True) + EPS)
            a = _silu(y * gl_ref[...] + bl_ref[...]).astype(BF16)
            o = _dot(a, w_ref[...]) + b_ref[...]
        else:
            o = _dot(a_ref[rows, :], w_ref[...])
        _mixer_epilogue(o, rows, *rest)


def _dense_out(a, w, x, mod, g2, wr, bias=None, ln=None):
    tm = 512
    e_in, e_out, e_shape = _epilogue_specs(tm)
    in_specs = [pl.BlockSpec((tm, D), lambda i: (i, 0)), pl.BlockSpec((D, D), lambda i: (0, 0))]
    args = [a, w]
    if ln is not None:
        in_specs += [pl.BlockSpec((1, D), lambda i: (0, 0))] * 3
        args += [bias.reshape(1, D), ln[0].reshape(1, D), ln[1].reshape(1, D)]
    return pl.pallas_call(
        functools.partial(_dense_out_kernel, layer_norm=ln is not None),
        grid=(N_TOK // tm,),
        in_specs=in_specs + e_in, out_specs=e_out, out_shape=e_shape,
        compiler_params=_params(1), name="dense_out",
    )(*args, x, mod, g2.reshape(1, D), wr)


def _head_rms(x, gain):
    w = x.shape[1]
    head_of_lane = lax.broadcasted_iota(jnp.int32, (w, LANES), 0) // HEAD_DIM
    gather = jnp.where(head_of_lane == lax.broadcasted_iota(jnp.int32, (w, LANES), 1), 1.0, 0.0).astype(BF16)
    head_of_col = lax.broadcasted_iota(jnp.int32, (LANES, w), 1) // HEAD_DIM
    spread = jnp.where(head_of_col == lax.broadcasted_iota(jnp.int32, (LANES, w), 0), 1.0, 0.0).astype(BF16)
    ms = _dot((x * x).astype(BF16), gather) * (1.0 / HEAD_DIM)
    hi, lo = _split2(lax.rsqrt(ms + EPS))
    return x * (_dot(hi, spread) + _dot(lo, spread)) * gain


def _rope(x, c, s):
    first = (lax.broadcasted_iota(jnp.int32, (1, LANES), 1) & 31) < 16
    outs = []
    for j in range(x.shape[1] // LANES):
        sl = slice(LANES * j, LANES * (j + 1))
        seg = x[:, sl]
        partner = jnp.where(first, pltpu.roll(seg, LANES - 16, 1), pltpu.roll(seg, 16, 1))
        outs.append(seg * c[:, sl] + partner * s[:, sl])
    return jnp.concatenate(outs, axis=1)


def _spread_kv(x, ones):
    lane = lax.broadcasted_iota(jnp.int32, (1, LANES), 1)
    low = lane < HEAD_DIM
    outs = []
    for kv in range(N_KV):
        pair = x[:, LANES * (kv // 2):LANES * (kv // 2 + 1)]
        swapped = pltpu.roll(pair, HEAD_DIM, 1)
        first, second = (swapped, pair) if kv % 2 else (pair, swapped)
        first = jnp.where(low, first, 1.0 if ones else 0.0)
        second = jnp.where(low, 1.0 if ones else 0.0, second)
        if ones:
            first = jnp.where(lane > HEAD_DIM, 0.0, first)
            second = jnp.where(jnp.logical_and(lane > 0, low), 0.0, second)
        outs += [first, second]
    return jnp.concatenate(outs, axis=1).astype(BF16)


def _qkv_kernel(h_ref, w_ref, gq_ref, gk_ref, ct_ref, st_ref, q_ref, k_ref, v_ref, kt_ref, vt_ref):
    nq, nk = N_HEADS * HEAD_DIM, N_KV * HEAD_DIM
    for rows in _row_slices(h_ref.shape[0], SUB_ROWS):
        qkv = _dot(h_ref[rows, :], w_ref[...])
        q = _head_rms(qkv[:, :nq], gq_ref[...])
        k = _head_rms(qkv[:, nq:nq + nk], gk_ref[...])
        v = qkv[:, nq + nk:]
        kt_ref[rows, :] = k
        vt_ref[rows, :] = v
        c, s = ct_ref[rows, :], st_ref[rows, :]
        q_ref[rows, :] = (_rope(q, c, s) * (HEAD_DIM ** -0.5 * math.log2(math.e))).astype(BF16)
        k_ref[rows, :] = _spread_kv(_rope(k, c[:, :nk], s[:, :nk]), ones=False)
        v_ref[rows, :] = _spread_kv(v, ones=True)


def _rope_tables(tm):
    n = SAMPLE_LEN
    row = jnp.repeat(jnp.arange(n // GRID_W), GRID_W).astype(F32)
    col = jnp.tile(jnp.arange(GRID_W), n // GRID_W).astype(F32)
    nf = HEAD_DIM // 4
    inv = ROPE_THETA ** (-jnp.arange(nf, dtype=F32) / nf)
    ar, ac = row[:, None] * inv, col[:, None] * inv
    cos = jnp.concatenate([jnp.cos(ar), jnp.cos(ar), jnp.cos(ac), jnp.cos(ac)], axis=1)
    sin = jnp.concatenate([-jnp.sin(ar), jnp.sin(ar), -jnp.sin(ac), jnp.sin(ac)], axis=1)
    cos = jnp.concatenate([jnp.ones((tm, HEAD_DIM), F32), cos], axis=0)
    sin = jnp.concatenate([jnp.zeros((tm, HEAD_DIM), F32), sin], axis=0)
    return jnp.tile(cos, (1, N_HEADS)), jnp.tile(sin, (1, N_HEADS))


def _attn_qkv(h, wqkv, gq, gk):
    tm = 512
    npt = N_PROMPT_TOK // tm
    ct, st = _rope_tables(tm)
    nq, nk = N_HEADS * HEAD_DIM, N_KV * HEAD_DIM

    def tab(i):
        return (jnp.where(i < npt, 0, 1 + (i - npt) % (SAMPLE_LEN // tm)), 0)

    row = lambda i: (i, 0)
    const = lambda i: (0, 0)
    return pl.pallas_call(
        _qkv_kernel,
        grid=(N_TOK // tm,),
        in_specs=[pl.BlockSpec((tm, D), row), pl.BlockSpec((D, nq + 2 * nk), const),
                  pl.BlockSpec((1, nq), const), pl.BlockSpec((1, nk), const),
                  pl.BlockSpec((tm, nq), tab), pl.BlockSpec((tm, nq), tab)],
        out_specs=[pl.BlockSpec((tm, nq), row), pl.BlockSpec((tm, 4 * nk), row), pl.BlockSpec((tm, 4 * nk), row),
                   pl.BlockSpec((tm, nk), row), pl.BlockSpec((tm, nk), row)],
        out_shape=[jax.ShapeDtypeStruct((N_TOK, nq), BF16),
                   jax.ShapeDtypeStruct((N_TOK, 4 * nk), BF16),
                   jax.ShapeDtypeStruct((N_TOK, 4 * nk), BF16),
                   jax.ShapeDtypeStruct((N_TOK, nk), F32),
                   jax.ShapeDtypeStruct((N_TOK, nk), F32)],
        compiler_params=_params(1), name="attn_qkv",
    )(h, wqkv, jnp.tile(gq, N_HEADS).reshape(1, nq), jnp.tile(gk, N_KV).reshape(1, nk), ct, st)


def _attend(q_ref, k_ref, v_ref, o_ref, cache, row0, n):
    qb = 256
    low = lax.broadcasted_iota(jnp.int32, (1, LANES), 1) < HEAD_DIM
    for kv in range(N_KV):
        base = 2 * LANES * kv
        operands = []
        for half in range(2):
            sl = slice(base + LANES * half, base + LANES * (half + 1))
            k, v = k_ref[row0:row0 + n, sl], v_ref[row0:row0 + n, sl]
            if cache is not None:
                k = jnp.concatenate([k, cache[0][:, sl]], axis=0)
                v = jnp.concatenate([v, cache[1][:, sl]], axis=0)
            operands.append((k, v))

        for r0 in range(row0, row0 + n, qb):
            q = jnp.concatenate([q_ref[pl.ds(r0, qb), base:base + LANES],
                                 q_ref[pl.ds(r0, qb), base + LANES:base + 2 * LANES]], axis=0)
            outs = []
            for half, (k, v) in enumerate(operands):
                s = _dot_nt(q, k)
                p = jnp.exp2(s - jnp.max(s, axis=1, keepdims=True))
                pv = _dot(p.astype(BF16), v)
                sum_col = 0 if half else HEAD_DIM
                outs.append(pv / pv[:, sum_col:sum_col + 1])
            o = jnp.where(low, outs[0], outs[1]).astype(BF16)
            o_ref[pl.ds(r0, qb), base:base + LANES] = o[:qb]
            o_ref[pl.ds(r0, qb), base + LANES:base + 2 * LANES] = o[qb:]


def _attn_kernel(q_ref, k_ref, v_ref, ck_ref, cv_ref, o_ref):
    step = pl.program_id(0)

    @pl.when(step < PROMPT_STEPS)
    def _():
        for r in range(STEP_TOK // PROMPT_LEN):
            _attend(q_ref, k_ref, v_ref, o_ref, None, r * PROMPT_LEN, PROMPT_LEN)

    @pl.when(step >= PROMPT_STEPS)
    def _():
        cache = (_spread_kv(ck_ref[0, 0], ones=False), _spread_kv(cv_ref[0, 0], ones=True))
        _attend(q_ref, k_ref, v_ref, o_ref, cache, 0, SAMPLE_LEN)


def _attention(q, k, v, cache_k, cache_v, layer_j):
    spec = pl.BlockSpec((STEP_TOK, D), lambda i: (i, 0))
    cspec = pl.BlockSpec((1, 1, PAST_LEN, N_KV * HEAD_DIM),
                         lambda i: (jnp.maximum(i - PROMPT_STEPS, 0), layer_j, 0, 0))
    return pl.pallas_call(
        _attn_kernel,
        grid=(N_TOK // STEP_TOK,),
        in_specs=[spec, spec, spec, cspec, cspec], out_specs=spec,
        out_shape=jax.ShapeDtypeStruct((N_TOK, D), BF16),
        compiler_params=_params(1), name="attention",
    )(q, k, v, cache_k, cache_v)


def _conv_fill(pad_ref, x, is_prompt):
    nslab, nblk = pad_ref.shape[0], pad_ref.shape[1]
    zeros = jnp.zeros((CONV_HALO, LANES), F32)
    top, bot = slice(0, CONV_HALO), slice(CONV_HALO + SUB_ROWS, 2 * CONV_HALO + SUB_ROWS)
    for s in range(nslab):
        for j in range(nblk):
            lanes = slice(LANES * j, LANES * (j + 1))
            pad_ref[s, j, CONV_HALO:CONV_HALO + SUB_ROWS, :] = x[s * SUB_ROWS:(s + 1) * SUB_ROWS, lanes]

    @pl.when(is_prompt)
    def _():
        for s in range(nslab):
            for j in range(nblk):
                pad_ref[s, j, top, :] = zeros
                pad_ref[s, j, bot, :] = zeros

    @pl.when(jnp.logical_not(is_prompt))
    def _():
        for s in range(nslab):
            for j in range(nblk):
                lanes = slice(LANES * j, LANES * (j + 1))
                r0 = s * SUB_ROWS
                pad_ref[s, j, top, :] = x[r0 - CONV_HALO:r0, lanes] if s else zeros
                pad_ref[s, j, bot, :] = (x[r0 + SUB_ROWS:r0 + SUB_ROWS + CONV_HALO, lanes]
                                         if s + 1 < nslab else zeros)


def _conv_rows(pad_ref, w, width, s):
    cols = []
    for j in range(pad_ref.shape[1]):
        acc = None
        for d in range(width):
            tap = pad_ref[s, j, pl.ds(CONV_HALO + d - width // 2, SUB_ROWS), :] * w[d:d + 1, LANES * j:LANES * (j + 1)]
            acc = tap if acc is None else acc + tap
        cols.append(acc)
    return cols[0] if len(cols) == 1 else jnp.concatenate(cols, axis=1)


def _log_sigmoid(x):
    return jnp.minimum(x, 0.0) - jnp.log1p(jnp.exp(-jnp.abs(x)))


def _ml_in_kernel(h_ref, wup_ref, wc_ref, wqkv_ref, wif_ref, bif_ref,
                  xc_ref, q_ref, k_ref, v_ref, gcol_ref, grow_ref, pad_ref, acc_ref):
    g = pl.program_id(1)
    xm = _dot(h_ref[...], wup_ref[...])
    _conv_fill(pad_ref, xm, pl.program_id(0) < PROMPT_STEPS)
    parts = []
    for s, rows in enumerate(_row_slices(xm.shape[0], SUB_ROWS)):
        xc = _silu(_conv_rows(pad_ref, wc_ref[...], ML_CONV, s))
        xc_ref[rows, :] = xc
        xcb = xc.astype(BF16)
        xmb = xm[rows].astype(BF16)
        part = None
        for t in range(wqkv_ref.shape[0]):
            cols = slice(ML_TILE * t, ML_TILE * (t + 1))
            q = _dot(xcb[:, cols], wqkv_ref[t, 0]).astype(BF16)
            k = (_dot(xcb[:, cols], wqkv_ref[t, 1]) * (ML_DH ** -0.5)).astype(BF16)
            v = _dot(xmb[:, cols], wqkv_ref[t, 2]).astype(BF16)
            q_ref[rows, cols] = q
            k_ref[rows, cols] = k
            v_ref[rows, cols] = v
            gates = _dot(q, wif_ref[0, t]) + _dot(k, wif_ref[1, t]) + _dot(v, wif_ref[2, t])
            part = gates if part is None else part + gates
        parts.append(part)
    part = jnp.concatenate(parts, axis=0)

    @pl.when(g == 0)
    def _():
        acc_ref[...] = part

    @pl.when(g > 0)
    def _():
        acc_ref[...] += part

    @pl.when(g == pl.num_programs(1) - 1)
    def _():
        gates = acc_ref[...] + bif_ref[...]
        lf = _log_sigmoid(gates)
        L = ML_CHUNK
        ti = lax.broadcasted_iota(jnp.int32, (L, L), 0)
        ui = lax.broadcasted_iota(jnp.int32, (L, L), 1)
        tri_f = jnp.where(ui <= ti, 1.0, 0.0).astype(BF16)
        tri_b = jnp.where(ui >= ti, 1.0, 0.0).astype(BF16)
        kind = lax.broadcasted_iota(jnp.int32, (1, LANES), 1) & 7
        rows = []
        for c in range(gates.shape[0] // L):
            p1, p2, p3 = _split3(lf[c * L:(c + 1) * L])
            bf = _dot(tri_f, p1) + _dot(tri_f, p2) + _dot(tri_f, p3)
            bb = _dot(tri_b, p1) + _dot(tri_b, p2) + _dot(tri_b, p3)
            rows.append(jnp.where(kind == 1, bf, jnp.where(kind == 3, bb, gates[c * L:(c + 1) * L])))
        out = jnp.concatenate(rows, axis=0)
        gcol_ref[...] = out
        grow_ref[...] = out.T


def _ml_in(h, wup, wconv, wqkv, wif, bif):
    tm, cg = STEP_TOK, 512
    tiles = cg // ML_TILE
    blk = lambda i, g: (i, g)
    return pl.pallas_call(
        _ml_in_kernel,
        grid=(N_TOK // tm, ML_INNER // cg),
        in_specs=[pl.BlockSpec((tm, D), lambda i, g: (i, 0)),
                  pl.BlockSpec((D, cg), lambda i, g: (0, g)),
                  pl.BlockSpec((ML_CONV, cg), lambda i, g: (0, g)),
                  pl.BlockSpec((tiles, 3, ML_TILE, ML_TILE), lambda i, g: (g, 0, 0, 0)),
                  pl.BlockSpec((3, tiles, ML_TILE, LANES), lambda i, g: (0, g, 0, 0)),
                  pl.BlockSpec((1, LANES), lambda i, g: (0, 0))],
        out_specs=[pl.BlockSpec((tm, cg), blk)] * 4 + [
            pl.BlockSpec((tm, LANES), lambda i, g: (i, 0)),
            pl.BlockSpec((LANES, tm), lambda i, g: (0, i))],
        out_shape=[jax.ShapeDtypeStruct((N_TOK, ML_INNER), F32),
                   jax.ShapeDtypeStruct((N_TOK, ML_INNER), BF16),
                   jax.ShapeDtypeStruct((N_TOK, ML_INNER), BF16),
                   jax.ShapeDtypeStruct((N_TOK, ML_INNER), BF16),
                   jax.ShapeDtypeStruct((N_TOK, LANES), F32),
                   jax.ShapeDtypeStruct((LANES, N_TOK), F32)],
        scratch_shapes=[pltpu.VMEM((tm // SUB_ROWS, cg // LANES, SUB_ROWS + 2 * CONV_HALO, LANES), F32),
                        pltpu.VMEM((tm, LANES), F32)],
        compiler_params=_params(2), name="mlstm_in",
    )(h, wup, wconv, wqkv, wif, bif)


def _scan_request(q_ref, k_ref, v_ref, gcol_ref, grow_ref, h_refs, hh, row0, n, carried, emit_state):
    L = ML_CHUNK
    nc = n // L
    lane = lax.broadcasted_iota(jnp.int32, (1, LANES), 1)
    ti = lax.broadcasted_iota(jnp.int32, (L, L), 0)
    si = lax.broadcasted_iota(jnp.int32, (L, L), 1)
    qk_single = None
    for d in range(2):
        if carried is not None:
            c0_ref, n0_ref, m0, c_sc, qk_sc = carried
            c_sc[...] = c0_ref[0, 0, d, 0]
            nvec = n0_ref[0, d, 0]
            m = jnp.full((1, 1), m0[d], F32)
        else:
            m = jnp.zeros((1, 1), F32)
        order = range(nc - 1, -1, -1) if d else range(nc)
        for step, c in enumerate(order):
            rows = slice(row0 + c * L, row0 + (c + 1) * L)
            qc, kc, vc = q_ref[rows, :], k_ref[rows, :], v_ref[rows, :]
            if nc == 1:
                if qk_single is None:
                    qk_single = _dot_nt(qc, kc)
                qk = qk_single
            elif d == 0:
                qk = _dot_nt(qc, kc)
                qk_sc[c] = qk
            else:
                qk = qk_sc[c]
            gc = gcol_ref[rows, :]

            def col(j, gc=gc):
                return jnp.sum(jnp.where(lane == 8 * hh + j, gc, 0.0), axis=1, keepdims=True)

            i_col, b_col = col(2 * d), col(2 * d + 1)
            i_row = grow_ref[2 * d:2 * d + 1, rows]
            b_row = grow_ref[2 * d + 1:2 * d + 2, rows]
            mask = (si >= ti) if d else (si <= ti)
            dm = jnp.where(mask, b_col - b_row + i_row, -jnp.inf)
            mt = jnp.maximum(b_col + m, jnp.max(dm, axis=1, keepdims=True))
            s = qk * jnp.exp(dm - mt)
            den = jnp.sum(s, axis=1, keepdims=True)
            num = _dot(s.astype(BF16), vc)
            if carried is not None:
                w_in = jnp.exp(b_col + m - mt)
                num = num + w_in * _dot_nt(qc, c_sc[...].astype(BF16))
                den = den + w_in * jnp.sum(qc.astype(F32) * nvec, axis=1, keepdims=True)
            h_refs[d][rows, :] = num / jnp.maximum(jnp.abs(den), jnp.exp(-mt))
            if carried is not None and step == nc - 1:
                continue
            edge = 0 if d else L - 1
            b_last = b_row[:, edge:edge + 1]
            m_new = mt[edge:edge + 1, :]
            kw = kc.astype(F32) * jnp.exp(b_last - b_col + i_col - m_new)
            upd = _dot_tn(vc, kw.astype(BF16))
            n_upd = jnp.sum(kw, axis=0, keepdims=True)
            if carried is not None:
                w_state = jnp.exp(b_last + m - m_new)
                c_sc[...] = w_state * c_sc[...] + upd
                nvec = w_state * nvec + n_upd
            else:
                emit_state(d, upd, n_upd, m_new)
            m = m_new


def _scan_kernel(q_ref, k_ref, v_ref, gcol_ref, grow_ref, c0_ref, n0_ref, m0_ref,
                 h0_ref, h1_ref, cn_ref, nn_ref, mn_ref, c_sc, qk_sc):
    hh = pl.program_id(0)
    step = pl.program_id(1)
    seqs = (q_ref, k_ref, v_ref, gcol_ref, grow_ref, (h0_ref, h1_ref), hh)

    @pl.when(step < PROMPT_STEPS)
    def _():
        mn_ref[...] = jnp.zeros(mn_ref.shape, F32)
        for r in range(STEP_TOK // PROMPT_LEN):
            def emit_state(d, c_new, n_new, m_new, r=r):
                cn_ref[r, 0, d, 0] = c_new
                nn_ref[r, d, 0] = n_new
                mn_ref[r, 0, d:d + 1, :] = jnp.broadcast_to(m_new, (1, LANES))

            _scan_request(*seqs, r * PROMPT_LEN, PROMPT_LEN, None, emit_state)

    @pl.when(step >= PROMPT_STEPS)
    def _():
        b = step - PROMPT_STEPS
        m0 = [m0_ref[b, d * ML_HEADS + hh] for d in range(2)]
        _scan_request(*seqs, 0, SAMPLE_LEN, (c0_ref, n0_ref, m0, c_sc, qk_sc), None)


def _ml_scan(q, k, v, gcol, grow, state_c, state_n, state_m):
    T = STEP_TOK
    per = T // PROMPT_LEN
    rb = lambda hh, i: (i, hh)
    sample = lambda i: jnp.maximum(i - PROMPT_STEPS, 0)
    prompt = lambda i: jnp.minimum(i, PROMPT_STEPS - 1)
    hshape = jax.ShapeDtypeStruct((N_TOK, ML_INNER), F32)
    h0, h1, new_c, new_n, new_m = pl.pallas_call(
        _scan_kernel,
        grid=(ML_HEADS, N_TOK // T),
        in_specs=[pl.BlockSpec((T, ML_DH), rb)] * 3 + [
            pl.BlockSpec((T, LANES), lambda hh, i: (i, 0)),
            pl.BlockSpec((8, T), lambda hh, i: (hh, i)),
            pl.BlockSpec((1, 1, 2, 1, ML_DH, ML_DH), lambda hh, i: (sample(i), 0, 0, hh, 0, 0)),
            pl.BlockSpec((1, 2, 1, 1, ML_DH), lambda hh, i: (sample(i), 0, hh, 0, 0)),
            pl.BlockSpec(memory_space=pltpu.SMEM)],
        out_specs=[pl.BlockSpec((T, ML_DH), rb)] * 2 + [
            pl.BlockSpec((per, 1, 2, 1, ML_DH, ML_DH), lambda hh, i: (prompt(i), 0, 0, hh, 0, 0)),
            pl.BlockSpec((per, 2, 1, 1, ML_DH), lambda hh, i: (prompt(i), 0, hh, 0, 0)),
            pl.BlockSpec((per, 1, 8, LANES), lambda hh, i: (prompt(i), hh, 0, 0))],
        out_shape=[hshape, hshape,
                   jax.ShapeDtypeStruct((N_PROMPT_REQ, 1, 2, ML_HEADS, ML_DH, ML_DH), F32),
                   jax.ShapeDtypeStruct((N_PROMPT_REQ, 2, ML_HEADS, 1, ML_DH), F32),
                   jax.ShapeDtypeStruct((N_PROMPT_REQ, ML_HEADS, 8, LANES), F32)],
        scratch_shapes=[pltpu.VMEM((ML_DH, ML_DH), F32),
                        pltpu.VMEM((SAMPLE_LEN // ML_CHUNK, ML_CHUNK, ML_CHUNK), F32)],
        compiler_params=_params(2), name="mlstm_scan",
    )(q, k, v, gcol, grow, state_c,
      state_n.reshape(N_SAMPLE_REQ, 2, ML_HEADS, 1, ML_DH), state_m.reshape(N_SAMPLE_REQ, 2 * ML_HEADS))
    new_n = new_n.reshape(N_PROMPT_REQ, 1, 2, ML_HEADS, ML_DH)
    new_m = jnp.transpose(new_m[:, :, :2, 0], (0, 2, 1)).reshape(N_PROMPT_REQ, 1, 2, ML_HEADS)
    return h0, h1, new_c, new_n, new_m


def _ml_out_kernel(h_ref, h0_ref, h1_ref, xc_ref, woz_ref, gout_ref, skip_ref, wd_ref, *rest):
    for rows in _row_slices(h_ref.shape[0], SUB_ROWS):
        h = h_ref[rows, :]
        acc = None
        for hd in range(ML_HEADS):
            sl = slice(ML_DH * hd, ML_DH * (hd + 1))
            o0 = _sigmoid(_dot(h, woz_ref[0, :, sl]))
            o1 = _sigmoid(_dot(h, woz_ref[1, :, sl]))
            z = _dot(h, woz_ref[2, :, sl])
            hs = o0 * h0_ref[rows, sl] + o1 * h1_ref[rows, sl]
            hn = hs * lax.rsqrt(jnp.mean(hs * hs, axis=1, keepdims=True) + EPS) * gout_ref[:, sl]
            y = ((hn + skip_ref[:, sl] * xc_ref[rows, sl]) * _silu(z)).astype(BF16)
            part = _dot(y, wd_ref[sl, :])
            acc = part if acc is None else acc + part
        _mixer_epilogue(acc, rows, *rest)


def _ml_out(h, h0, h1, xc, woz, gout, skip, wdown, x, mod, g2, wr):
    tm = 512
    e_in, e_out, e_shape = _epilogue_specs(tm)
    row = lambda i: (i, 0)
    const = lambda i: (0, 0)
    once = pl.Buffered(1)
    return pl.pallas_call(
        _ml_out_kernel,
        grid=(N_TOK // tm,),
        in_specs=[pl.BlockSpec((tm, D), row), pl.BlockSpec((tm, ML_INNER), row),
                  pl.BlockSpec((tm, ML_INNER), row), pl.BlockSpec((tm, ML_INNER), row),
                  pl.BlockSpec((3, D, ML_INNER), lambda i: (0, 0, 0), pipeline_mode=once),
                  pl.BlockSpec((1, ML_INNER), const), pl.BlockSpec((1, ML_INNER), const),
                  pl.BlockSpec((ML_INNER, D), const, pipeline_mode=once)] + e_in,
        out_specs=e_out, out_shape=e_shape,
        compiler_params=_params(1), name="mlstm_out",
    )(h, h0, h1, xc, woz, gout.reshape(1, ML_INNER), skip.reshape(1, ML_INNER), wdown,
      x, mod, g2.reshape(1, D), wr)


def _conf_in_kernel(h_ref, wa_ref, wg_ref, ba_ref, bg_ref, wdw_ref, bdw_ref, u_ref, pad_ref):
    h = h_ref[...]
    a = _dot(h, wa_ref[...]) + ba_ref[...]
    g = _dot(h, wg_ref[...]) + bg_ref[...]
    _conv_fill(pad_ref, a * _sigmoid(g), pl.program_id(0) < PROMPT_STEPS)
    for s, rows in enumerate(_row_slices(h.shape[0], SUB_ROWS)):
        u_ref[rows, :] = _conv_rows(pad_ref, wdw_ref[...], CONV_WIDTH, s) + bdw_ref[...]


def _conf_in(h, w1, b1, wdw, bdw):
    tm, cg = STEP_TOK, 256
    ngroups = D // cg
    b1 = b1.reshape(1, 2 * D)
    return pl.pallas_call(
        _conf_in_kernel,
        grid=(N_TOK // tm, ngroups),
        in_specs=[pl.BlockSpec((tm, D), lambda i, g: (i, 0)),
                  pl.BlockSpec((D, cg), lambda i, g: (0, g)),
                  pl.BlockSpec((D, cg), lambda i, g: (0, ngroups + g)),
                  pl.BlockSpec((1, cg), lambda i, g: (0, g)),
                  pl.BlockSpec((1, cg), lambda i, g: (0, ngroups + g)),
                  pl.BlockSpec((CONV_WIDTH, cg), lambda i, g: (0, g)),
                  pl.BlockSpec((1, cg), lambda i, g: (0, g))],
        out_specs=pl.BlockSpec((tm, cg), lambda i, g: (i, g)),
        out_shape=jax.ShapeDtypeStruct((N_TOK, D), F32),
        scratch_shapes=[pltpu.VMEM((tm // SUB_ROWS, cg // LANES, SUB_ROWS + 2 * CONV_HALO, LANES), F32)],
        compiler_params=_params(2), name="conformer_in",
    )(h, w1, w1, b1, b1, wdw, bdw.reshape(1, D))


def _route_kernel(aff_ref, slot_ref, slott_ref, *, nreq, n, cap):
    a = jnp.concatenate([aff_ref[:, n * r:n * (r + 1)] for r in range(nreq)], axis=0)
    capf = float(cap)

    def bisect(i, p):
        cand = p | jnp.left_shift(jnp.int32(1), 30 - i)
        cnt = jnp.sum(jnp.where(a >= pltpu.bitcast(cand, F32), 1.0, 0.0), axis=1, keepdims=True)
        return jnp.where(cnt >= capf, cand, p)

    thr = pltpu.bitcast(lax.fori_loop(0, 31, bisect, jnp.zeros((a.shape[0], 1), jnp.int32)), F32)
    gt = a > thr
    eq = a == thr
    need = capf - jnp.sum(jnp.where(gt, 1.0, 0.0), axis=1, keepdims=True)
    before = jnp.where(lax.broadcasted_iota(jnp.int32, (n, n), 0) < lax.broadcasted_iota(jnp.int32, (n, n), 1),
                       1.0, 0.0).astype(BF16)
    eq_rank = _dot(jnp.where(eq, 1.0, 0.0).astype(BF16), before)
    sel = jnp.logical_or(gt, jnp.logical_and(eq, eq_rank < need))
    pos = _dot(jnp.where(sel, 1.0, 0.0).astype(BF16), before)
    slot = jnp.where(sel, pos, -1.0)
    slot_ref[...] = slot
    slott_ref[...] = slot.T


def _route(aff, cfg):
    nreq, n, cap = cfg["nreq"], cfg["n"], cfg["cap"]
    rows = nreq * N_EXPERTS
    half = cfg["tok0"] // (nreq * n)
    return pl.pallas_call(
        functools.partial(_route_kernel, nreq=nreq, n=n, cap=cap),
        grid=(1,),
        in_specs=[pl.BlockSpec((N_EXPERTS, nreq * n), lambda i: (0, half))],
        out_specs=[pl.BlockSpec((rows, n), lambda i: (0, 0)), pl.BlockSpec((n, rows), lambda i: (0, 0))],
        out_shape=[jax.ShapeDtypeStruct((rows, n), F32), jax.ShapeDtypeStruct((n, rows), F32)],
        compiler_params=_params(1), name="moe_route",
    )(aff)


def _gather_rows(h, slot, aff, cap):
    n = slot.shape[1]
    ci = lax.broadcasted_iota(jnp.int32, (cap, n), 0).astype(F32)
    onehots, vals = [], []
    for e in range(N_EXPERTS):
        hit = slot[e:e + 1, :] == ci
        onehots.append(jnp.where(hit, 1.0, 0.0).astype(BF16))
        val = jnp.sum(jnp.where(hit, aff[e:e + 1, :], 0.0), axis=1, keepdims=True)
        vals.append(jnp.broadcast_to(val, (cap, LANES)))
    return _dot(jnp.concatenate(onehots, axis=0), h).astype(BF16), vals


def _gather_kernel(h_ref, slotp_ref, slots_ref, aff_ref, xs_ref, vals_ref):
    step = pl.program_id(0)

    @pl.when(step < PROMPT_STEPS)
    def _():
        n, cap = PROMPT_LEN, CAP_PROMPT
        for r in range(STEP_TOK // n):
            xs, vals = _gather_rows(h_ref[n * r:n * (r + 1), :], slotp_ref[N_EXPERTS * r:N_EXPERTS * (r + 1), :],
                                    aff_ref[:, n * r:n * (r + 1)], cap)
            xs_ref[:, cap * r:cap * (r + 1), :] = xs.reshape(N_EXPERTS, cap, D)
            for e in range(N_EXPERTS):
                vals_ref[e, cap * r:cap * (r + 1), :] = vals[e]

    @pl.when(step >= PROMPT_STEPS)
    def _():
        xs, vals = _gather_rows(h_ref[...], slots_ref[...], aff_ref[...], CAP_SAMPLE)
        xs_ref[...] = xs.reshape(N_EXPERTS, CAP_SAMPLE, D)
        for e in range(N_EXPERTS):
            vals_ref[e] = vals[e]


def _gather(h2, aff, slots):
    slot_p, slot_s = slots
    per = STEP_TOK // PROMPT_LEN
    return pl.pallas_call(
        _gather_kernel,
        grid=(N_TOK // STEP_TOK,),
        in_specs=[pl.BlockSpec((STEP_TOK, D), lambda i: (i, 0)),
                  pl.BlockSpec((per * N_EXPERTS, PROMPT_LEN), lambda i: (jnp.minimum(i, PROMPT_STEPS - 1), 0)),
                  pl.BlockSpec((N_EXPERTS, SAMPLE_LEN), lambda i: (jnp.maximum(i - PROMPT_STEPS, 0), 0)),
                  pl.BlockSpec((N_EXPERTS, STEP_TOK), lambda i: (0, i))],
        out_specs=[pl.BlockSpec((N_EXPERTS, STEP_SLOTS, D), lambda i: (0, i, 0)),
                   pl.BlockSpec((N_EXPERTS, STEP_SLOTS, LANES), lambda i: (0, i, 0))],
        out_shape=[jax.ShapeDtypeStruct((N_EXPERTS, ROWS_PER_EXPERT, D), BF16),
                   jax.ShapeDtypeStruct((N_EXPERTS, ROWS_PER_EXPERT, LANES), F32)],
        compiler_params=_params(1), name="moe_gather",
    )(h2, slot_p, slot_s, aff)


def _ffn_kernel(xs_ref, vals_ref, wg_ref, wu_ref, wd_ref, ys_ref):
    wg = wg_ref[0, 0].astype(BF16)
    wu = wu_ref[0, 0].astype(BF16)
    wd = wd_ref[0, 0].astype(BF16)
    for rows in _row_slices(xs_ref.shape[1], 2 * SUB_ROWS):
        xs = xs_ref[0, rows, :]
        act = (_silu(_dot(xs, wg)) * _dot(xs, wu)).astype(BF16)
        ys = _dot(act, wd)
        vals = vals_ref[0, rows, :]
        ys_ref[0, rows, :] = jnp.concatenate(
            [ys[:, LANES * j:LANES * (j + 1)] * vals for j in range(D // LANES)], axis=1).astype(BF16)


def _ffn(xs, vals, wg, wu, wd, layer):
    wspec = pl.BlockSpec((1, 1, D, D), lambda e: (layer, e, 0, 0))
    rows = lambda e: (e, 0, 0)
    return pl.pallas_call(
        _ffn_kernel,
        grid=(N_EXPERTS,),
        in_specs=[pl.BlockSpec((1, ROWS_PER_EXPERT, D), rows),
                  pl.BlockSpec((1, ROWS_PER_EXPERT, LANES), rows),
                  wspec, wspec, wspec],
        out_specs=pl.BlockSpec((1, ROWS_PER_EXPERT, D), rows),
        out_shape=jax.ShapeDtypeStruct((N_EXPERTS, ROWS_PER_EXPERT, D), BF16),
        compiler_params=_params(1), name="moe_ffn",
    )(xs, vals, wg, wu, wd)


def _scatter_rows(slott, ys, req, cap):
    st = slott.astype(BF16)
    rb, width = st.shape[1], N_EXPERTS * cap
    ri = lax.broadcasted_iota(jnp.int32, (rb, width), 0)
    ji = lax.broadcasted_iota(jnp.int32, (rb, width), 1)
    expand = jnp.where(ri == N_EXPERTS * req + (ji >> int(math.log2(cap))), 1.0, 0.0).astype(BF16)
    want = (lax.broadcasted_iota(jnp.int32, (1, width), 1) & (cap - 1)).astype(F32)
    scatter = jnp.where(_dot(st, expand) == want, 1.0, 0.0).astype(BF16)
    return _dot(scatter, ys)


def _combine_kernel(*refs, final):
    slottp_ref, slotts_ref, ys_ref, x_ref, mod_ref = refs[:5]
    step = pl.program_id(0)
    gate = mod_ref[0][5:6]
    if final:
        g_ref, yp_ref, ysm_ref = refs[5:]
    else:
        modn_ref, g_ref, xo_ref, h_ref = refs[5:]

    def emit(rows, x_new, prompt):
        if final:
            y = x_new * lax.rsqrt(jnp.mean(x_new * x_new, axis=1, keepdims=True) + EPS) * g_ref[...]
            (yp_ref if prompt else ysm_ref)[rows, :] = y
        else:
            xo_ref[rows, :] = x_new
            mn = modn_ref[0]
            h_ref[rows, :] = _modulated_norm(x_new, g_ref[...], mn[0:1], mn[1:2]).astype(BF16)

    @pl.when(step < PROMPT_STEPS)
    def _():
        n, cap = PROMPT_LEN, CAP_PROMPT
        per = STEP_TOK // n
        for r in range(per):
            rows = slice(n * r, n * (r + 1))
            ys = ys_ref[:, cap * r:cap * (r + 1), :].reshape(N_EXPERTS * cap, D)
            y = _scatter_rows(slottp_ref[...], ys, step * per + r, cap)
            emit(rows, x_ref[rows, :] + gate * y, True)

    @pl.when(step >= PROMPT_STEPS)
    def _():
        ys = ys_ref[...].reshape(N_EXPERTS * CAP_SAMPLE, D)
        y = _scatter_rows(slotts_ref[...], ys, step - PROMPT_STEPS, CAP_SAMPLE)
        emit(slice(0, STEP_TOK), x_ref[...] + gate * y, False)


def _combine(slotts, ys, x, mod, g_next, mod_next=None):
    final = mod_next is None
    slott_p, slott_s = slotts
    row = lambda i: (i, 0)
    cond = lambda i: (_cond_index(i, STEP_TOK), 0, 0)
    in_specs = [pl.BlockSpec(slott_p.shape, lambda i: (0, 0)),
                pl.BlockSpec(slott_s.shape, lambda i: (0, 0)),
                pl.BlockSpec((N_EXPERTS, STEP_SLOTS, D), lambda i: (0, i, 0)),
                pl.BlockSpec((STEP_TOK, D), row),
                pl.BlockSpec((1, 6, D), cond)]
    args = [slott_p, slott_s, ys, x, mod]
    if final:
        in_specs.append(pl.BlockSpec((1, D), lambda i: (0, 0)))
        args.append(g_next.reshape(1, D))
        out_specs = [pl.BlockSpec((STEP_TOK, D), lambda i: (jnp.minimum(i, PROMPT_STEPS - 1), 0)),
                     pl.BlockSpec((STEP_TOK, D), lambda i: (jnp.maximum(i - PROMPT_STEPS, 0), 0))]
        out_shape = [jax.ShapeDtypeStruct((N_PROMPT_TOK, D), F32),
                     jax.ShapeDtypeStruct((N_TOK - N_PROMPT_TOK, D), F32)]
    else:
        in_specs += [pl.BlockSpec((1, 6, D), cond), pl.BlockSpec((1, D), lambda i: (0, 0))]
        args += [mod_next, g_next.reshape(1, D)]
        out_specs = [pl.BlockSpec((STEP_TOK, D), row)] * 2
        out_shape = [jax.ShapeDtypeStruct((N_TOK, D), F32), jax.ShapeDtypeStruct((N_TOK, D), BF16)]
    return pl.pallas_call(
        functools.partial(_combine_kernel, final=final),
        grid=(N_TOK // STEP_TOK,),
        in_specs=in_specs, out_specs=out_specs, out_shape=out_shape,
        compiler_params=_params(1), name="moe_combine",
    )(*args)


def _moe(x, h2, aff, mod, moe_w, layer, g_next, mod_next):
    slot_p, slott_p = _route(aff, PROMPT)
    slot_s, slott_s = _route(aff, SAMPLE)
    xs, vals = _gather(h2, aff, (slot_p, slot_s))
    ys = _ffn(xs, vals, *moe_w, layer)
    return _combine((slott_p, slott_s), ys, x, mod, g_next, mod_next)


def _block_diag_tiles(w):
    g = w.reshape(ML_INNER // ML_TILE, ML_TILE // 4, 4, 4)
    eye = jnp.eye(ML_TILE // 4, dtype=w.dtype)
    return jnp.einsum("gnio,nm->gnimo", g, eye).reshape(ML_INNER // ML_TILE, ML_TILE, ML_TILE)


def _gate_weights(w_if, b_if):
    H = ML_HEADS
    cols = jnp.stack([w_if[0][:, :H], w_if[0][:, H:], w_if[1][:, :H], w_if[1][:, H:]], axis=-1)
    cols = jnp.pad(cols, ((0, 0), (0, 0), (0, 4))).reshape(3 * ML_INNER, 8 * H)
    w = jnp.pad(cols, ((0, 0), (0, LANES - 8 * H))).astype(BF16).reshape(3, ML_INNER // ML_TILE, ML_TILE, LANES)
    b = jnp.stack([b_if[0][:H], b_if[0][H:], b_if[1][:H], b_if[1][H:]], axis=-1)
    b = jnp.pad(jnp.pad(b, ((0, 0), (0, 4))).reshape(1, 8 * H), ((0, 0), (0, LANES - 8 * H)))
    return w, b


def kernel(x_prompt, x_sample, cache_k, cache_v, state_C, state_n, state_m, c, c_ctx, w_ada, b_ada, g_norm1, g_norm2, attn_wq, attn_wk, attn_wv, attn_wo, attn_gq, attn_gk, ml_w_up, ml_conv, ml_wq, ml_wk, ml_wv, ml_w_if, ml_b_if, ml_w_o, ml_g_out, ml_skip, ml_w_down, cv_w_pw1, cv_b_pw1, cv_w_dw, cv_b_dw, cv_g_ln, cv_b_ln, cv_w_pw2, cv_b_pw2, moe_router, moe_w_gate, moe_w_up, moe_w_down, g_final):
    cond = jnp.zeros((16, D), F32).at[0].set(c_ctx).at[1:1 + N_SAMPLE_REQ].set(c)
    mods = _ada_table(cond, w_ada, b_ada).reshape(DEPTH, 16, 6, D)
    cache_k = cache_k.reshape(N_SAMPLE_REQ, -1, PAST_LEN, N_KV * HEAD_DIM)
    cache_v = cache_v.reshape(N_SAMPLE_REQ, -1, PAST_LEN, N_KV * HEAD_DIM)

    x, h = _prenorm(x_prompt.reshape(N_PROMPT_TOK, D), x_sample.reshape(-1, D), mods[0], g_norm1[0])
    new_k, new_v = [], []
    for layer in range(DEPTH):
        kind, j = layer % 3, layer // 3
        mod = mods[layer]
        wr = jnp.pad(moe_router[layer], ((0, 0), (0, LANES - N_EXPERTS)))
        if kind == 0:
            wqkv = jnp.concatenate([attn_wq[j], attn_wk[j], attn_wv[j]], axis=1).astype(BF16)
            q, k, v, k_tok, v_tok = _attn_qkv(h, wqkv, attn_gq[j], attn_gk[j])
            new_k.append(k_tok[:N_PROMPT_TOK].reshape(N_PROMPT_REQ, PROMPT_LEN, N_KV, HEAD_DIM))
            new_v.append(v_tok[:N_PROMPT_TOK].reshape(N_PROMPT_REQ, PROMPT_LEN, N_KV, HEAD_DIM))
            o = _attention(q, k, v, cache_k, cache_v, j)
            x, h2, aff = _dense_out(o, attn_wo[j].astype(BF16), x, mod, g_norm2[layer], wr)
        elif kind == 1:
            wup = ml_w_up[j].astype(BF16)
            wqkv = jnp.stack([_block_diag_tiles(ml_wq[j]), _block_diag_tiles(ml_wk[j]),
                              _block_diag_tiles(ml_wv[j])], axis=1).astype(BF16)
            wif, bif = _gate_weights(ml_w_if[j], ml_b_if[j])
            xc, q, k, v, gcol, grow = _ml_in(h, wup[:, :ML_INNER], ml_conv[j], wqkv, wif, bif)
            h0, h1, new_c, new_n, new_m = _ml_scan(q, k, v, gcol, grow, state_C, state_n, state_m)
            woz = jnp.stack([ml_w_o[j][0].astype(BF16), ml_w_o[j][1].astype(BF16), wup[:, ML_INNER:]])
            x, h2, aff = _ml_out(h, h0, h1, xc, woz, ml_g_out[j], ml_skip[j], ml_w_down[j].astype(BF16),
                                 x, mod, g_norm2[layer], wr)
        else:
            u = _conf_in(h, cv_w_pw1[j].astype(BF16), cv_b_pw1[j], cv_w_dw[j], cv_b_dw[j])
            x, h2, aff = _dense_out(u, cv_w_pw2[j].astype(BF16), x, mod, g_norm2[layer], wr,
                                    bias=cv_b_pw2[j], ln=(cv_g_ln[j], cv_b_ln[j]))
        moe_w = (moe_w_gate, moe_w_up, moe_w_down)
        if layer + 1 < DEPTH:
            x, h = _moe(x, h2, aff, mod, moe_w, layer, g_norm1[layer + 1], mods[layer + 1])
        else:
            y_prompt, y_sample = _moe(x, h2, aff, mod, moe_w, layer, g_final, None)
    return (y_prompt.reshape(N_PROMPT_REQ, PROMPT_LEN, D), y_sample.reshape(N_SAMPLE_REQ, SAMPLE_LEN, D),
            jnp.stack(new_k, axis=1), jnp.stack(new_v, axis=1), new_c, new_n, new_m)
```

```python
import functools
import math

import jax
import jax.numpy as jnp
from jax import lax
from jax.experimental import pallas as pl
from jax.experimental.pallas import tpu as pltpu

F32 = jnp.float32
BF16 = jnp.bfloat16

D = 1024
N_PROMPT_REQ, PROMPT_LEN = 32, 256
N_SAMPLE_REQ, SAMPLE_LEN = 8, 1024
N_PROMPT_TOK = N_PROMPT_REQ * PROMPT_LEN
N_TOK = N_PROMPT_TOK + N_SAMPLE_REQ * SAMPLE_LEN
DEPTH = 4
GRID_W = 64
N_HEADS, N_KV, HEAD_DIM = 16, 4, 64
PAST_LEN = 512
ROPE_THETA = 10000.0
ML_INNER, ML_HEADS, ML_DH = 2048, 4, 512
ML_CONV, ML_CHUNK = 5, 256
ML_TILE = 256
CONV_WIDTH = 31
CONV_HALO = 16
N_EXPERTS = 16
CAP_PROMPT = 2 * PROMPT_LEN // N_EXPERTS
CAP_SAMPLE = 2 * SAMPLE_LEN // N_EXPERTS
ROWS_PER_EXPERT = N_PROMPT_REQ * CAP_PROMPT + N_SAMPLE_REQ * CAP_SAMPLE
EPS = 1e-6
LANES = 128
VMEM_LIMIT = 56 * 1024 * 1024
SUB_ROWS = 256
STEP_TOK = SAMPLE_LEN
PROMPT_STEPS = N_PROMPT_TOK // STEP_TOK
STEP_SLOTS = CAP_SAMPLE

PROMPT = dict(nreq=N_PROMPT_REQ, n=PROMPT_LEN, cap=CAP_PROMPT, tok0=0)
SAMPLE = dict(nreq=N_SAMPLE_REQ, n=SAMPLE_LEN, cap=CAP_SAMPLE, tok0=N_PROMPT_TOK)


def _params(n_axes):
    return pltpu.CompilerParams(dimension_semantics=("arbitrary",) * n_axes,
                                vmem_limit_bytes=VMEM_LIMIT)


def _cond_index(i, tm):
    npt = N_PROMPT_TOK // tm
    return jnp.where(i < npt, 0, 1 + (i - npt) // (SAMPLE_LEN // tm))


def _row_slices(total, size):
    return [slice(r, r + size) for r in range(0, total, size)]


def _sigmoid(x):
    return 1.0 / (1.0 + jnp.exp(-x))


def _silu(x):
    return x * _sigmoid(x)


def _dot(a, b):
    return jnp.dot(a, b, preferred_element_type=F32)


def _dot_nt(a, b):
    return lax.dot_general(a, b, (((1,), (1,)), ((), ())), preferred_element_type=F32)


def _dot_tn(a, b):
    return lax.dot_general(a, b, (((0,), (0,)), ((), ())), preferred_element_type=F32)


def _split2(x):
    hi = x.astype(BF16)
    return hi, (x - hi.astype(F32)).astype(BF16)


def _split3(x):
    p1 = x.astype(BF16)
    r1 = x - p1.astype(F32)
    p2 = r1.astype(BF16)
    p3 = (r1 - p2.astype(F32)).astype(BF16)
    return p1, p2, p3


def _modulated_norm(x, g, shift, scale):
    ms = jnp.mean(x * x, axis=1, keepdims=True)
    return x * lax.rsqrt(ms + EPS) * g * (1.0 + scale) + shift


def _ada_kernel(c_ref, w_ref, b_ref, o_ref):
    s = _silu(c_ref[...]).astype(BF16)
    o_ref[0] = _dot(s, w_ref[0].astype(BF16)) + b_ref[0]


def _ada_table(cond, w_ada, b_ada):
    return pl.pallas_call(
        _ada_kernel,
        grid=(DEPTH, 6),
        in_specs=[pl.BlockSpec((16, D), lambda l, j: (0, 0)),
                  pl.BlockSpec((1, D, D), lambda l, j: (l, 0, j)),
                  pl.BlockSpec((1, 1, D), lambda l, j: (l, 0, j))],
        out_specs=pl.BlockSpec((1, 16, D), lambda l, j: (l, 0, j)),
        out_shape=jax.ShapeDtypeStruct((DEPTH, 16, 6 * D), F32),
        compiler_params=_params(2), name="ada_table",
    )(cond, w_ada, b_ada.reshape(DEPTH, 1, 6 * D))


def _prenorm_kernel(xp_ref, xs_ref, mod_ref, g_ref, x_ref, h_ref, *, npt):
    m = mod_ref[0]

    def emit(src_ref):
        x = src_ref[...]
        x_ref[...] = x
        h_ref[...] = _modulated_norm(x, g_ref[...], m[0:1], m[1:2]).astype(BF16)

    @pl.when(pl.program_id(0) < npt)
    def _():
        emit(xp_ref)

    @pl.when(pl.program_id(0) >= npt)
    def _():
        emit(xs_ref)


def _prenorm(x_prompt, x_sample, mod, g):
    tm = 512
    npt = N_PROMPT_TOK // tm
    row = lambda i: (i, 0)
    return pl.pallas_call(
        functools.partial(_prenorm_kernel, npt=npt),
        grid=(N_TOK // tm,),
        in_specs=[pl.BlockSpec((tm, D), lambda i: (jnp.minimum(i, npt - 1), 0)),
                  pl.BlockSpec((tm, D), lambda i: (jnp.maximum(i - npt, 0), 0)),
                  pl.BlockSpec((1, 6, D), lambda i: (_cond_index(i, tm), 0, 0)),
                  pl.BlockSpec((1, D), lambda i: (0, 0))],
        out_specs=[pl.BlockSpec((tm, D), row), pl.BlockSpec((tm, D), row)],
        out_shape=[jax.ShapeDtypeStruct((N_TOK, D), F32), jax.ShapeDtypeStruct((N_TOK, D), BF16)],
        compiler_params=_params(1), name="prenorm",
    )(x_prompt, x_sample, mod, g.reshape(1, D))


def _mixer_epilogue(o, rows, x_ref, mod_ref, g2_ref, wr_ref, xo_ref, h2_ref, aff_ref):
    m = mod_ref[0]
    x_new = x_ref[rows, :] + m[2:3] * o
    xo_ref[rows, :] = x_new
    h2 = _modulated_norm(x_new, g2_ref[...], m[3:4], m[4:5])
    hh = h2.astype(BF16)
    h2_ref[rows, :] = hh
    hl = (h2 - hh.astype(F32)).astype(BF16)
    wr = wr_ref[...]
    wh = wr.astype(BF16)
    wl = (wr - wh.astype(F32)).astype(BF16)
    both = _dot(hh, jnp.concatenate([wh, wl], axis=1))
    logits = both[:, :LANES] + both[:, LANES:] + _dot(hl, wh)
    lt = logits.T[:N_EXPERTS]
    e = jnp.exp(lt - jnp.max(lt, axis=0, keepdims=True))
    aff_ref[:, rows] = e / jnp.sum(e, axis=0, keepdims=True)


def _epilogue_specs(tm):
    in_specs = [pl.BlockSpec((tm, D), lambda i: (i, 0)),
                pl.BlockSpec((1, 6, D), lambda i: (_cond_index(i, tm), 0, 0)),
                pl.BlockSpec((1, D), lambda i: (0, 0)),
                pl.BlockSpec((D, LANES), lambda i: (0, 0))]
    out_specs = [pl.BlockSpec((tm, D), lambda i: (i, 0)),
                 pl.BlockSpec((tm, D), lambda i: (i, 0)),
                 pl.BlockSpec((N_EXPERTS, tm), lambda i: (0, i))]
    out_shape = [jax.ShapeDtypeStruct((N_TOK, D), F32),
                 jax.ShapeDtypeStruct((N_TOK, D), BF16),
                 jax.ShapeDtypeStruct((N_EXPERTS, N_TOK), F32)]
    return in_specs, out_specs, out_shape


def _dense_out_kernel(*refs, layer_norm):
    a_ref, w_ref = refs[:2]
    rest = refs[5:] if layer_norm else refs[2:]
    for rows in _row_slices(a_ref.shape[0], SUB_ROWS):
        if layer_norm:
            b_ref, gl_ref, bl_ref = refs[2:5]
            u = a_ref[rows, :]
            mu = jnp.mean(u, axis=1, keepdims=True)
            uc = u - mu
            y = uc * lax.rsqrt(jnp.mean(uc * uc, axis=1, keepdims=True) + EPS)
            a = _silu(y * gl_ref[...] + bl_ref[...]).astype(BF16)
            o = _dot(a, w_ref[...]) + b_ref[...]
        else:
            o = _dot(a_ref[rows, :], w_ref[...])
        _mixer_epilogue(o, rows, *rest)


def _dense_out(a, w, x, mod, g2, wr, bias=None, ln=None):
    tm = 512
    e_in, e_out, e_shape = _epilogue_specs(tm)
    in_specs = [pl.BlockSpec((tm, D), lambda i: (i, 0)), pl.BlockSpec((D, D), lambda i: (0, 0))]
    args = [a, w]
    if ln is not None:
        in_specs += [pl.BlockSpec((1, D), lambda i: (0, 0))] * 3
        args += [bias.reshape(1, D), ln[0].reshape(1, D), ln[1].reshape(1, D)]
    return pl.pallas_call(
        functools.partial(_dense_out_kernel, layer_norm=ln is not None),
        grid=(N_TOK // tm,),
        in_specs=in_specs + e_in, out_specs=e_out, out_shape=e_shape,
        compiler_params=_params(1), name="dense_out",
    )(*args, x, mod, g2.reshape(1, D), wr)


def _head_rms(x, gain):
    w = x.shape[1]
    head_of_lane = lax.broadcasted_iota(jnp.int32, (w, LANES), 0) // HEAD_DIM
    gather = jnp.where(head_of_lane == lax.broadcasted_iota(jnp.int32, (w, LANES), 1), 1.0, 0.0).astype(BF16)
    head_of_col = lax.broadcasted_iota(jnp.int32, (LANES, w), 1) // HEAD_DIM
    spread = jnp.where(head_of_col == lax.broadcasted_iota(jnp.int32, (LANES, w), 0), 1.0, 0.0).astype(BF16)
    ms = _dot((x * x).astype(BF16), gather) * (1.0 / HEAD_DIM)
    hi, lo = _split2(lax.rsqrt(ms + EPS))
    return x * (_dot(hi, spread) + _dot(lo, spread)) * gain


def _rope(x, c, s):
    first = (lax.broadcasted_iota(jnp.int32, (1, LANES), 1) & 31) < 16
    outs = []
    for j in range(x.shape[1] // LANES):
        sl = slice(LANES * j, LANES * (j + 1))
        seg = x[:, sl]
        partner = jnp.where(first, pltpu.roll(seg, LANES - 16, 1), pltpu.roll(seg, 16, 1))
        outs.append(seg * c[:, sl] + partner * s[:, sl])
    return jnp.concatenate(outs, axis=1)


def _spread_kv(x, ones):
    lane = lax.broadcasted_iota(jnp.int32, (1, LANES), 1)
    low = lane < HEAD_DIM
    outs = []
    for kv in range(N_KV):
        pair = x[:, LANES * (kv // 2):LANES * (kv // 2 + 1)]
        swapped = pltpu.roll(pair, HEAD_DIM, 1)
        first, second = (swapped, pair) if kv % 2 else (pair, swapped)
        first = jnp.where(low, first, 1.0 if ones else 0.0)
        second = jnp.where(low, 1.0 if ones else 0.0, second)
        if ones:
            first = jnp.where(lane > HEAD_DIM, 0.0, first)
            second = jnp.where(jnp.logical_and(lane > 0, low), 0.0, second)
        outs += [first, second]
    return jnp.concatenate(outs, axis=1).astype(BF16)


def _qkv_kernel(h_ref, w_ref, gq_ref, gk_ref, ct_ref, st_ref, q_ref, k_ref, v_ref, kt_ref, vt_ref):
    nq, nk = N_HEADS * HEAD_DIM, N_KV * HEAD_DIM
    for rows in _row_slices(h_ref.shape[0], SUB_ROWS):
        qkv = _dot(h_ref[rows, :], w_ref[...])
        q = _head_rms(qkv[:, :nq], gq_ref[...])
        k = _head_rms(qkv[:, nq:nq + nk], gk_ref[...])
        v = qkv[:, nq + nk:]
        kt_ref[rows, :] = k
        vt_ref[rows, :] = v
        c, s = ct_ref[rows, :], st_ref[rows, :]
        q_ref[rows, :] = (_rope(q, c, s) * (HEAD_DIM ** -0.5 * math.log2(math.e))).astype(BF16)
        k_ref[rows, :] = _spread_kv(_rope(k, c[:, :nk], s[:, :nk]), ones=False)
        v_ref[rows, :] = _spread_kv(v, ones=True)


def _rope_tables(tm):
    n = SAMPLE_LEN
    row = jnp.repeat(jnp.arange(n // GRID_W), GRID_W).astype(F32)
    col = jnp.tile(jnp.arange(GRID_W), n // GRID_W).astype(F32)
    nf = HEAD_DIM // 4
    inv = ROPE_THETA ** (-jnp.arange(nf, dtype=F32) / nf)
    ar, ac = row[:, None] * inv, col[:, None] * inv
    cos = jnp.concatenate([jnp.cos(ar), jnp.cos(ar), jnp.cos(ac), jnp.cos(ac)], axis=1)
    sin = jnp.concatenate([-jnp.sin(ar), jnp.sin(ar), -jnp.sin(ac), jnp.sin(ac)], axis=1)
    cos = jnp.concatenate([jnp.ones((tm, HEAD_DIM), F32), cos], axis=0)
    sin = jnp.concatenate([jnp.zeros((tm, HEAD_DIM), F32), sin], axis=0)
    return jnp.tile(cos, (1, N_HEADS)), jnp.tile(sin, (1, N_HEADS))


def _attn_qkv(h, wqkv, gq, gk):
    tm = 512
    npt = N_PROMPT_TOK // tm
    ct, st = _rope_tables(tm)
    nq, nk = N_HEADS * HEAD_DIM, N_KV * HEAD_DIM

    def tab(i):
        return (jnp.where(i < npt, 0, 1 + (i - npt) % (SAMPLE_LEN // tm)), 0)

    row = lambda i: (i, 0)
    const = lambda i: (0, 0)
    return pl.pallas_call(
        _qkv_kernel,
        grid=(N_TOK // tm,),
        in_specs=[pl.BlockSpec((tm, D), row), pl.BlockSpec((D, nq + 2 * nk), const),
                  pl.BlockSpec((1, nq), const), pl.BlockSpec((1, nk), const),
                  pl.BlockSpec((tm, nq), tab), pl.BlockSpec((tm, nq), tab)],
        out_specs=[pl.BlockSpec((tm, nq), row), pl.BlockSpec((tm, 4 * nk), row), pl.BlockSpec((tm, 4 * nk), row),
                   pl.BlockSpec((tm, nk), row), pl.BlockSpec((tm, nk), row)],
        out_shape=[jax.ShapeDtypeStruct((N_TOK, nq), BF16),
                   jax.ShapeDtypeStruct((N_TOK, 4 * nk), BF16),
                   jax.ShapeDtypeStruct((N_TOK, 4 * nk), BF16),
                   jax.ShapeDtypeStruct((N_TOK, nk), F32),
                   jax.ShapeDtypeStruct((N_TOK, nk), F32)],
        compiler_params=_params(1), name="attn_qkv",
    )(h, wqkv, jnp.tile(gq, N_HEADS).reshape(1, nq), jnp.tile(gk, N_KV).reshape(1, nk), ct, st)


def _attend(q_ref, k_ref, v_ref, o_ref, cache, row0, n):
    qb = 256
    low = lax.broadcasted_iota(jnp.int32, (1, LANES), 1) < HEAD_DIM
    for kv in range(N_KV):
        base = 2 * LANES * kv
        operands = []
        for half in range(2):
            sl = slice(base + LANES * half, base + LANES * (half + 1))
            k, v = k_ref[row0:row0 + n, sl], v_ref[row0:row0 + n, sl]
            if cache is not None:
                k = jnp.concatenate([k, cache[0][:, sl]], axis=0)
                v = jnp.concatenate([v, cache[1][:, sl]], axis=0)
            operands.append((k, v))

        def block(i, carry, base=base, operands=operands):
            r0 = row0 + i * qb if isinstance(i, int) else pl.multiple_of(row0 + i * qb, qb)
            q = jnp.concatenate([q_ref[pl.ds(r0, qb), base:base + LANES],
                                 q_ref[pl.ds(r0, qb), base + LANES:base + 2 * LANES]], axis=0)
            outs = []
            for half, (k, v) in enumerate(operands):
                s = _dot_nt(q, k)
                p = jnp.exp2(s - jnp.max(s, axis=1, keepdims=True))
                pv = _dot(p.astype(BF16), v)
                sum_col = 0 if half else HEAD_DIM
                outs.append(pv / pv[:, sum_col:sum_col + 1])
            o = jnp.where(low, outs[0], outs[1]).astype(BF16)
            o_ref[pl.ds(r0, qb), base:base + LANES] = o[:qb]
            o_ref[pl.ds(r0, qb), base + LANES:base + 2 * LANES] = o[qb:]
            return carry

        if n == qb:
            block(0, 0)
        else:
            lax.fori_loop(0, n // qb, block, 0, unroll=2)


def _attn_kernel(q_ref, k_ref, v_ref, ck_ref, cv_ref, o_ref):
    step = pl.program_id(0)

    @pl.when(step < PROMPT_STEPS)
    def _():
        for r in range(STEP_TOK // PROMPT_LEN):
            _attend(q_ref, k_ref, v_ref, o_ref, None, r * PROMPT_LEN, PROMPT_LEN)

    @pl.when(step >= PROMPT_STEPS)
    def _():
        cache = (_spread_kv(ck_ref[0, 0], ones=False), _spread_kv(cv_ref[0, 0], ones=True))
        _attend(q_ref, k_ref, v_ref, o_ref, cache, 0, SAMPLE_LEN)


def _attention(q, k, v, cache_k, cache_v, layer_j):
    spec = pl.BlockSpec((STEP_TOK, D), lambda i: (i, 0))
    cspec = pl.BlockSpec((1, 1, PAST_LEN, N_KV * HEAD_DIM),
                         lambda i: (jnp.maximum(i - PROMPT_STEPS, 0), layer_j, 0, 0))
    return pl.pallas_call(
        _attn_kernel,
        grid=(N_TOK // STEP_TOK,),
        in_specs=[spec, spec, spec, cspec, cspec], out_specs=spec,
        out_shape=jax.ShapeDtypeStruct((N_TOK, D), BF16),
        compiler_params=_params(1), name="attention",
    )(q, k, v, cache_k, cache_v)


def _conv_fill(pad_ref, x, is_prompt):
    nslab, nblk = pad_ref.shape[0], pad_ref.shape[1]
    zeros = jnp.zeros((CONV_HALO, LANES), F32)
    top, bot = slice(0, CONV_HALO), slice(CONV_HALO + SUB_ROWS, 2 * CONV_HALO + SUB_ROWS)
    for s in range(nslab):
        for j in range(nblk):
            lanes = slice(LANES * j, LANES * (j + 1))
            pad_ref[s, j, CONV_HALO:CONV_HALO + SUB_ROWS, :] = x[s * SUB_ROWS:(s + 1) * SUB_ROWS, lanes]

    @pl.when(is_prompt)
    def _():
        for s in range(nslab):
            for j in range(nblk):
                pad_ref[s, j, top, :] = zeros
                pad_ref[s, j, bot, :] = zeros

    @pl.when(jnp.logical_not(is_prompt))
    def _():
        for s in range(nslab):
            for j in range(nblk):
                lanes = slice(LANES * j, LANES * (j + 1))
                r0 = s * SUB_ROWS
                pad_ref[s, j, top, :] = x[r0 - CONV_HALO:r0, lanes] if s else zeros
                pad_ref[s, j, bot, :] = (x[r0 + SUB_ROWS:r0 + SUB_ROWS + CONV_HALO, lanes]
                                         if s + 1 < nslab else zeros)


def _conv_rows(pad_ref, w, width, s):
    cols = []
    for j in range(pad_ref.shape[1]):
        acc = None
        for d in range(width):
            tap = pad_ref[s, j, pl.ds(CONV_HALO + d - width // 2, SUB_ROWS), :] * w[d:d + 1, LANES * j:LANES * (j + 1)]
            acc = tap if acc is None else acc + tap
        cols.append(acc)
    return cols[0] if len(cols) == 1 else jnp.concatenate(cols, axis=1)


def _log_sigmoid(x):
    return jnp.minimum(x, 0.0) - jnp.log1p(jnp.exp(-jnp.abs(x)))


def _ml_in_kernel(h_ref, wup_ref, wc_ref, wqkv_ref, wif_ref, bif_ref,
                  xc_ref, q_ref, k_ref, v_ref, gcol_ref, grow_ref, pad_ref, acc_ref):
    g = pl.program_id(1)
    xm = _dot(h_ref[...], wup_ref[...])
    _conv_fill(pad_ref, xm, pl.program_id(0) < PROMPT_STEPS)
    parts = []
    for s, rows in enumerate(_row_slices(xm.shape[0], SUB_ROWS)):
        xcb = _silu(_conv_rows(pad_ref, wc_ref[...], ML_CONV, s)).astype(BF16)
        xc_ref[rows, :] = xcb
        xmb = xm[rows].astype(BF16)
        part = None
        for t in range(wqkv_ref.shape[0]):
            cols = slice(ML_TILE * t, ML_TILE * (t + 1))
            q = _dot(xcb[:, cols], wqkv_ref[t, 0]).astype(BF16)
            k = (_dot(xcb[:, cols], wqkv_ref[t, 1]) * (ML_DH ** -0.5)).astype(BF16)
            v = _dot(xmb[:, cols], wqkv_ref[t, 2]).astype(BF16)
            q_ref[rows, cols] = q
            k_ref[rows, cols] = k
            v_ref[rows, cols] = v
            gates = _dot(q, wif_ref[0, t]) + _dot(k, wif_ref[1, t]) + _dot(v, wif_ref[2, t])
            part = gates if part is None else part + gates
        parts.append(part)
    part = jnp.concatenate(parts, axis=0)

    @pl.when(g == 0)
    def _():
        acc_ref[...] = part

    @pl.when(g > 0)
    def _():
        acc_ref[...] += part

    @pl.when(g == pl.num_programs(1) - 1)
    def _():
        gates = acc_ref[...] + bif_ref[...]
        lf = _log_sigmoid(gates)
        L = ML_CHUNK
        ti = lax.broadcasted_iota(jnp.int32, (L, L), 0)
        ui = lax.broadcasted_iota(jnp.int32, (L, L), 1)
        tri_f = jnp.where(ui <= ti, 1.0, 0.0).astype(BF16)
        tri_b = jnp.where(ui >= ti, 1.0, 0.0).astype(BF16)
        kind = lax.broadcasted_iota(jnp.int32, (1, LANES), 1) & 7
        rows = []
        for c in range(gates.shape[0] // L):
            p1, p2, p3 = _split3(lf[c * L:(c + 1) * L])
            bf = _dot(tri_f, p1) + _dot(tri_f, p2) + _dot(tri_f, p3)
            bb = _dot(tri_b, p1) + _dot(tri_b, p2) + _dot(tri_b, p3)
            rows.append(jnp.where(kind == 1, bf, jnp.where(kind == 3, bb, gates[c * L:(c + 1) * L])))
        out = jnp.concatenate(rows, axis=0)
        gcol_ref[...] = out
        grow_ref[...] = out.T


def _ml_in(h, wup, wconv, wqkv, wif, bif):
    tm, cg = STEP_TOK, 512
    tiles = cg // ML_TILE
    blk = lambda i, g: (i, g)
    return pl.pallas_call(
        _ml_in_kernel,
        grid=(N_TOK // tm, ML_INNER // cg),
        in_specs=[pl.BlockSpec((tm, D), lambda i, g: (i, 0)),
                  pl.BlockSpec((D, cg), lambda i, g: (0, g)),
                  pl.BlockSpec((ML_CONV, cg), lambda i, g: (0, g)),
                  pl.BlockSpec((tiles, 3, ML_TILE, ML_TILE), lambda i, g: (g, 0, 0, 0)),
                  pl.BlockSpec((3, tiles, ML_TILE, LANES), lambda i, g: (0, g, 0, 0)),
                  pl.BlockSpec((1, LANES), lambda i, g: (0, 0))],
        out_specs=[pl.BlockSpec((tm, cg), blk)] * 4 + [
            pl.BlockSpec((tm, LANES), lambda i, g: (i, 0)),
            pl.BlockSpec((LANES, tm), lambda i, g: (0, i))],
        out_shape=[jax.ShapeDtypeStruct((N_TOK, ML_INNER), BF16),
                   jax.ShapeDtypeStruct((N_TOK, ML_INNER), BF16),
                   jax.ShapeDtypeStruct((N_TOK, ML_INNER), BF16),
                   jax.ShapeDtypeStruct((N_TOK, ML_INNER), BF16),
                   jax.ShapeDtypeStruct((N_TOK, LANES), F32),
                   jax.ShapeDtypeStruct((LANES, N_TOK), F32)],
        scratch_shapes=[pltpu.VMEM((tm // SUB_ROWS, cg // LANES, SUB_ROWS + 2 * CONV_HALO, LANES), F32),
                        pltpu.VMEM((tm, LANES), F32)],
        compiler_params=_params(2), name="mlstm_in",
    )(h, wup, wconv, wqkv, wif, bif)


def _scan_request(q_ref, k_ref, v_ref, gcol_ref, grow_ref, h_refs, hh, row0, n, carried, emit_state):
    L = ML_CHUNK
    nc = n // L
    lane = lax.broadcasted_iota(jnp.int32, (1, LANES), 1)
    ti = lax.broadcasted_iota(jnp.int32, (L, L), 0)
    si = lax.broadcasted_iota(jnp.int32, (L, L), 1)
    qk_single = None
    for d in range(2):
        if carried is not None:
            c0_ref, n0_ref, m0, c_sc, qk_sc = carried
            c_sc[...] = c0_ref[0, 0, d, 0]
            nvec = n0_ref[0, d, 0]
            m = jnp.full((1, 1), m0[d], F32)
        else:
            m = jnp.zeros((1, 1), F32)
        order = range(nc - 1, -1, -1) if d else range(nc)
        for step, c in enumerate(order):
            rows = slice(row0 + c * L, row0 + (c + 1) * L)
            qc, kc, vc = q_ref[rows, :], k_ref[rows, :], v_ref[rows, :]
            if nc == 1:
                if qk_single is None:
                    qk_single = _dot_nt(qc, kc)
                qk = qk_single
            elif d == 0:
                qk = _dot_nt(qc, kc)
                qk_sc[c] = qk
            else:
                qk = qk_sc[c]
            gc = gcol_ref[rows, :]

            def col(j, gc=gc):
                return jnp.sum(jnp.where(lane == 8 * hh + j, gc, 0.0), axis=1, keepdims=True)

            i_col, b_col = col(2 * d), col(2 * d + 1)
            i_row = grow_ref[2 * d:2 * d + 1, rows]
            b_row = grow_ref[2 * d + 1:2 * d + 2, rows]
            mask = (si >= ti) if d else (si <= ti)
            dm = jnp.where(mask, b_col - b_row + i_row, -jnp.inf)
            mt = jnp.maximum(b_col + m, jnp.max(dm, axis=1, keepdims=True))
            s = qk * jnp.exp(dm - mt)
            den = jnp.sum(s, axis=1, keepdims=True)
            num = _dot(s.astype(BF16), vc)
            if carried is not None:
                w_in = jnp.exp(b_col + m - mt)
                num = num + w_in * _dot_nt(qc, c_sc[...].astype(BF16))
                den = den + w_in * jnp.sum(qc.astype(F32) * nvec, axis=1, keepdims=True)
            h_refs[d][rows, :] = (num / jnp.maximum(jnp.abs(den), jnp.exp(-mt))).astype(BF16)
            if carried is not None and step == nc - 1:
                continue
            edge = 0 if d else L - 1
            b_last = b_row[:, edge:edge + 1]
            m_new = mt[edge:edge + 1, :]
            kw = kc.astype(F32) * jnp.exp(b_last - b_col + i_col - m_new)
            upd = _dot_tn(vc, kw.astype(BF16))
            n_upd = jnp.sum(kw, axis=0, keepdims=True)
            if carried is not None:
                w_state = jnp.exp(b_last + m - m_new)
                c_sc[...] = w_state * c_sc[...] + upd
                nvec = w_state * nvec + n_upd
            else:
                emit_state(d, upd, n_upd, m_new)
            m = m_new


def _scan_kernel(q_ref, k_ref, v_ref, gcol_ref, grow_ref, c0_ref, n0_ref, m0_ref,
                 h0_ref, h1_ref, cn_ref, nn_ref, mn_ref, c_sc, qk_sc):
    hh = pl.program_id(0)
    step = pl.program_id(1)
    seqs = (q_ref, k_ref, v_ref, gcol_ref, grow_ref, (h0_ref, h1_ref), hh)

    @pl.when(step < PROMPT_STEPS)
    def _():
        mn_ref[...] = jnp.zeros(mn_ref.shape, F32)
        for r in range(STEP_TOK // PROMPT_LEN):
            def emit_state(d, c_new, n_new, m_new, r=r):
                cn_ref[r, 0, d, 0] = c_new
                nn_ref[r, d, 0] = n_new
                mn_ref[r, 0, d:d + 1, :] = jnp.broadcast_to(m_new, (1, LANES))

            _scan_request(*seqs, r * PROMPT_LEN, PROMPT_LEN, None, emit_state)

    @pl.when(step >= PROMPT_STEPS)
    def _():
        b = step - PROMPT_STEPS
        m0 = [m0_ref[b, d * ML_HEADS + hh] for d in range(2)]
        _scan_request(*seqs, 0, SAMPLE_LEN, (c0_ref, n0_ref, m0, c_sc, qk_sc), None)


def _ml_scan(q, k, v, gcol, grow, state_c, state_n, state_m):
    T = STEP_TOK
    per = T // PROMPT_LEN
    rb = lambda hh, i: (i, hh)
    sample = lambda i: jnp.maximum(i - PROMPT_STEPS, 0)
    prompt = lambda i: jnp.minimum(i, PROMPT_STEPS - 1)
    hshape = jax.ShapeDtypeStruct((N_TOK, ML_INNER), BF16)
    h0, h1, new_c, new_n, new_m = pl.pallas_call(
        _scan_kernel,
        grid=(ML_HEADS, N_TOK // T),
        in_specs=[pl.BlockSpec((T, ML_DH), rb)] * 3 + [
            pl.BlockSpec((T, LANES), lambda hh, i: (i, 0)),
            pl.BlockSpec((8, T), lambda hh, i: (hh, i)),
            pl.BlockSpec((1, 1, 2, 1, ML_DH, ML_DH), lambda hh, i: (sample(i), 0, 0, hh, 0, 0)),
            pl.BlockSpec((1, 2, 1, 1, ML_DH), lambda hh, i: (sample(i), 0, hh, 0, 0)),
            pl.BlockSpec(memory_space=pltpu.SMEM)],
        out_specs=[pl.BlockSpec((T, ML_DH), rb)] * 2 + [
            pl.BlockSpec((per, 1, 2, 1, ML_DH, ML_DH), lambda hh, i: (prompt(i), 0, 0, hh, 0, 0)),
            pl.BlockSpec((per, 2, 1, 1, ML_DH), lambda hh, i: (prompt(i), 0, hh, 0, 0)),
            pl.BlockSpec((per, 1, 8, LANES), lambda hh, i: (prompt(i), hh, 0, 0))],
        out_shape=[hshape, hshape,
                   jax.ShapeDtypeStruct((N_PROMPT_REQ, 1, 2, ML_HEADS, ML_DH, ML_DH), F32),
                   jax.ShapeDtypeStruct((N_PROMPT_REQ, 2, ML_HEADS, 1, ML_DH), F32),
                   jax.ShapeDtypeStruct((N_PROMPT_REQ, ML_HEADS, 8, LANES), F32)],
        scratch_shapes=[pltpu.VMEM((ML_DH, ML_DH), F32),
                        pltpu.VMEM((SAMPLE_LEN // ML_CHUNK, ML_CHUNK, ML_CHUNK), F32)],
        compiler_params=_params(2), name="mlstm_scan",
    )(q, k, v, gcol, grow, state_c,
      state_n.reshape(N_SAMPLE_REQ, 2, ML_HEADS, 1, ML_DH), state_m.reshape(N_SAMPLE_REQ, 2 * ML_HEADS))
    new_n = new_n.reshape(N_PROMPT_REQ, 1, 2, ML_HEADS, ML_DH)
    new_m = jnp.transpose(new_m[:, :, :2, 0], (0, 2, 1)).reshape(N_PROMPT_REQ, 1, 2, ML_HEADS)
    return h0, h1, new_c, new_n, new_m


def _ml_out_kernel(h_ref, h0_ref, h1_ref, xc_ref, woz_ref, gout_ref, skip_ref, wd_ref, *rest):
    for rows in _row_slices(h_ref.shape[0], SUB_ROWS):
        h = h_ref[rows, :]
        acc = None
        for hd in range(ML_HEADS):
            sl = slice(ML_DH * hd, ML_DH * (hd + 1))
            o0 = _sigmoid(_dot(h, woz_ref[0, :, sl]))
            o1 = _sigmoid(_dot(h, woz_ref[1, :, sl]))
            z = _dot(h, woz_ref[2, :, sl])
            hs = o0 * h0_ref[rows, sl] + o1 * h1_ref[rows, sl]
            hn = hs * lax.rsqrt(jnp.mean(hs * hs, axis=1, keepdims=True) + EPS) * gout_ref[:, sl]
            y = ((hn + skip_ref[:, sl] * xc_ref[rows, sl]) * _silu(z)).astype(BF16)
            part = _dot(y, wd_ref[sl, :])
            acc = part if acc is None else acc + part
        _mixer_epilogue(acc, rows, *rest)


def _ml_out(h, h0, h1, xc, woz, gout, skip, wdown, x, mod, g2, wr):
    tm = 512
    e_in, e_out, e_shape = _epilogue_specs(tm)
    row = lambda i: (i, 0)
    const = lambda i: (0, 0)
    once = pl.Buffered(1)
    return pl.pallas_call(
        _ml_out_kernel,
        grid=(N_TOK // tm,),
        in_specs=[pl.BlockSpec((tm, D), row), pl.BlockSpec((tm, ML_INNER), row),
                  pl.BlockSpec((tm, ML_INNER), row), pl.BlockSpec((tm, ML_INNER), row),
                  pl.BlockSpec((3, D, ML_INNER), lambda i: (0, 0, 0), pipeline_mode=once),
                  pl.BlockSpec((1, ML_INNER), const), pl.BlockSpec((1, ML_INNER), const),
                  pl.BlockSpec((ML_INNER, D), const, pipeline_mode=once)] + e_in,
        out_specs=e_out, out_shape=e_shape,
        compiler_params=_params(1), name="mlstm_out",
    )(h, h0, h1, xc, woz, gout.reshape(1, ML_INNER), skip.reshape(1, ML_INNER), wdown,
      x, mod, g2.reshape(1, D), wr)


def _conf_in_kernel(h_ref, wa_ref, wg_ref, ba_ref, bg_ref, wdw_ref, bdw_ref, u_ref, pad_ref):
    h = h_ref[...]
    a = _dot(h, wa_ref[...]) + ba_ref[...]
    g = _dot(h, wg_ref[...]) + bg_ref[...]
    _conv_fill(pad_ref, a * _sigmoid(g), pl.program_id(0) < PROMPT_STEPS)
    for s, rows in enumerate(_row_slices(h.shape[0], SUB_ROWS)):
        u_ref[rows, :] = _conv_rows(pad_ref, wdw_ref[...], CONV_WIDTH, s) + bdw_ref[...]


def _conf_in(h, w1, b1, wdw, bdw):
    tm, cg = STEP_TOK, 256
    ngroups = D // cg
    b1 = b1.reshape(1, 2 * D)
    return pl.pallas_call(
        _conf_in_kernel,
        grid=(N_TOK // tm, ngroups),
        in_specs=[pl.BlockSpec((tm, D), lambda i, g: (i, 0)),
                  pl.BlockSpec((D, cg), lambda i, g: (0, g)),
                  pl.BlockSpec((D, cg), lambda i, g: (0, ngroups + g)),
                  pl.BlockSpec((1, cg), lambda i, g: (0, g)),
                  pl.BlockSpec((1, cg), lambda i, g: (0, ngroups + g)),
                  pl.BlockSpec((CONV_WIDTH, cg), lambda i, g: (0, g)),
                  pl.BlockSpec((1, cg), lambda i, g: (0, g))],
        out_specs=pl.BlockSpec((tm, cg), lambda i, g: (i, g)),
        out_shape=jax.ShapeDtypeStruct((N_TOK, D), F32),
        scratch_shapes=[pltpu.VMEM((tm // SUB_ROWS, cg // LANES, SUB_ROWS + 2 * CONV_HALO, LANES), F32)],
        compiler_params=_params(2), name="conformer_in",
    )(h, w1, w1, b1, b1, wdw, bdw.reshape(1, D))


def _route_kernel(aff_ref, slot_ref, slott_ref, *, nreq, n, cap):
    a = jnp.concatenate([aff_ref[:, n * r:n * (r + 1)] for r in range(nreq)], axis=0)
    capf = float(cap)

    def bisect(i, p):
        cand = p | jnp.left_shift(jnp.int32(1), 30 - i)
        cnt = jnp.sum(jnp.where(a >= pltpu.bitcast(cand, F32), 1.0, 0.0), axis=1, keepdims=True)
        return jnp.where(cnt >= capf, cand, p)

    thr = pltpu.bitcast(lax.fori_loop(0, 31, bisect, jnp.zeros((a.shape[0], 1), jnp.int32)), F32)
    gt = a > thr
    eq = a == thr
    need = capf - jnp.sum(jnp.where(gt, 1.0, 0.0), axis=1, keepdims=True)
    before = jnp.where(lax.broadcasted_iota(jnp.int32, (n, n), 0) < lax.broadcasted_iota(jnp.int32, (n, n), 1),
                       1.0, 0.0).astype(BF16)
    eq_rank = _dot(jnp.where(eq, 1.0, 0.0).astype(BF16), before)
    sel = jnp.logical_or(gt, jnp.logical_and(eq, eq_rank < need))
    pos = _dot(jnp.where(sel, 1.0, 0.0).astype(BF16), before)
    slot = jnp.where(sel, pos, -1.0)
    slot_ref[...] = slot
    slott_ref[...] = slot.T


def _route(aff, cfg):
    nreq, n, cap = cfg["nreq"], cfg["n"], cfg["cap"]
    rows = nreq * N_EXPERTS
    half = cfg["tok0"] // (nreq * n)
    return pl.pallas_call(
        functools.partial(_route_kernel, nreq=nreq, n=n, cap=cap),
        grid=(1,),
        in_specs=[pl.BlockSpec((N_EXPERTS, nreq * n), lambda i: (0, half))],
        out_specs=[pl.BlockSpec((rows, n), lambda i: (0, 0)), pl.BlockSpec((n, rows), lambda i: (0, 0))],
        out_shape=[jax.ShapeDtypeStruct((rows, n), F32), jax.ShapeDtypeStruct((n, rows), F32)],
        compiler_params=_params(1), name="moe_route",
    )(aff)


def _gather_rows(h, slot, aff, cap):
    n = slot.shape[1]
    ci = lax.broadcasted_iota(jnp.int32, (cap, n), 0).astype(F32)
    onehots, vals = [], []
    for e in range(N_EXPERTS):
        hit = slot[e:e + 1, :] == ci
        onehots.append(jnp.where(hit, 1.0, 0.0).astype(BF16))
        val = jnp.sum(jnp.where(hit, aff[e:e + 1, :], 0.0), axis=1, keepdims=True)
        vals.append(jnp.broadcast_to(val, (cap, LANES)))
    return _dot(jnp.concatenate(onehots, axis=0), h).astype(BF16), vals


def _gather_kernel(h_ref, slotp_ref, slots_ref, aff_ref, xs_ref, vals_ref):
    step = pl.program_id(0)

    @pl.when(step < PROMPT_STEPS)
    def _():
        n, cap = PROMPT_LEN, CAP_PROMPT
        for r in range(STEP_TOK // n):
            xs, vals = _gather_rows(h_ref[n * r:n * (r + 1), :], slotp_ref[N_EXPERTS * r:N_EXPERTS * (r + 1), :],
                                    aff_ref[:, n * r:n * (r + 1)], cap)
            xs_ref[:, cap * r:cap * (r + 1), :] = xs.reshape(N_EXPERTS, cap, D)
            for e in range(N_EXPERTS):
                vals_ref[e, cap * r:cap * (r + 1), :] = vals[e]

    @pl.when(step >= PROMPT_STEPS)
    def _():
        xs, vals = _gather_rows(h_ref[...], slots_ref[...], aff_ref[...], CAP_SAMPLE)
        xs_ref[...] = xs.reshape(N_EXPERTS, CAP_SAMPLE, D)
        for e in range(N_EXPERTS):
            vals_ref[e] = vals[e]


def _gather(h2, aff, slots):
    slot_p, slot_s = slots
    per = STEP_TOK // PROMPT_LEN
    return pl.pallas_call(
        _gather_kernel,
        grid=(N_TOK // STEP_TOK,),
        in_specs=[pl.BlockSpec((STEP_TOK, D), lambda i: (i, 0)),
                  pl.BlockSpec((per * N_EXPERTS, PROMPT_LEN), lambda i: (jnp.minimum(i, PROMPT_STEPS - 1), 0)),
                  pl.BlockSpec((N_EXPERTS, SAMPLE_LEN), lambda i: (jnp.maximum(i - PROMPT_STEPS, 0), 0)),
                  pl.BlockSpec((N_EXPERTS, STEP_TOK), lambda i: (0, i))],
        out_specs=[pl.BlockSpec((N_EXPERTS, STEP_SLOTS, D), lambda i: (0, i, 0)),
                   pl.BlockSpec((N_EXPERTS, STEP_SLOTS, LANES), lambda i: (0, i, 0))],
        out_shape=[jax.ShapeDtypeStruct((N_EXPERTS, ROWS_PER_EXPERT, D), BF16),
                   jax.ShapeDtypeStruct((N_EXPERTS, ROWS_PER_EXPERT, LANES), F32)],
        compiler_params=_params(1), name="moe_gather",
    )(h2, slot_p, slot_s, aff)


def _ffn_kernel(xs_ref, vals_ref, wg_ref, wu_ref, wd_ref, ys_ref):
    wg = wg_ref[0, 0].astype(BF16)
    wu = wu_ref[0, 0].astype(BF16)
    wd = wd_ref[0, 0].astype(BF16)
    for rows in _row_slices(xs_ref.shape[1], 2 * SUB_ROWS):
        xs = xs_ref[0, rows, :]
        act = (_silu(_dot(xs, wg)) * _dot(xs, wu)).astype(BF16)
        ys = _dot(act, wd)
        vals = vals_ref[0, rows, :]
        ys_ref[0, rows, :] = jnp.concatenate(
            [ys[:, LANES * j:LANES * (j + 1)] * vals for j in range(D // LANES)], axis=1).astype(BF16)


def _ffn(xs, vals, wg, wu, wd, layer):
    wspec = pl.BlockSpec((1, 1, D, D), lambda e: (layer, e, 0, 0))
    rows = lambda e: (e, 0, 0)
    return pl.pallas_call(
        _ffn_kernel,
        grid=(N_EXPERTS,),
        in_specs=[pl.BlockSpec((1, ROWS_PER_EXPERT, D), rows),
                  pl.BlockSpec((1, ROWS_PER_EXPERT, LANES), rows),
                  wspec, wspec, wspec],
        out_specs=pl.BlockSpec((1, ROWS_PER_EXPERT, D), rows),
        out_shape=jax.ShapeDtypeStruct((N_EXPERTS, ROWS_PER_EXPERT, D), BF16),
        compiler_params=_params(1), name="moe_ffn",
    )(xs, vals, wg, wu, wd)


def _scatter_rows(slott, ys, req, cap):
    st = slott.astype(BF16)
    rb, width = st.shape[1], N_EXPERTS * cap
    ri = lax.broadcasted_iota(jnp.int32, (rb, width), 0)
    ji = lax.broadcasted_iota(jnp.int32, (rb, width), 1)
    expand = jnp.where(ri == N_EXPERTS * req + (ji >> int(math.log2(cap))), 1.0, 0.0).astype(BF16)
    want = (lax.broadcasted_iota(jnp.int32, (1, width), 1) & (cap - 1)).astype(F32)
    scatter = jnp.where(_dot(st, expand) == want, 1.0, 0.0).astype(BF16)
    return _dot(scatter, ys)


def _combine_kernel(*refs, final):
    slottp_ref, slotts_ref, ys_ref, x_ref, mod_ref = refs[:5]
    step = pl.program_id(0)
    gate = mod_ref[0][5:6]
    if final:
        g_ref, yp_ref, ysm_ref = refs[5:]
    else:
        modn_ref, g_ref, xo_ref, h_ref = refs[5:]

    def emit(rows, x_new, prompt):
        if final:
            y = x_new * lax.rsqrt(jnp.mean(x_new * x_new, axis=1, keepdims=True) + EPS) * g_ref[...]
            (yp_ref if prompt else ysm_ref)[rows, :] = y
        else:
            xo_ref[rows, :] = x_new
            mn = modn_ref[0]
            h_ref[rows, :] = _modulated_norm(x_new, g_ref[...], mn[0:1], mn[1:2]).astype(BF16)

    @pl.when(step < PROMPT_STEPS)
    def _():
        n, cap = PROMPT_LEN, CAP_PROMPT
        per = STEP_TOK // n
        for r in range(per):
            rows = slice(n * r, n * (r + 1))
            ys = ys_ref[:, cap * r:cap * (r + 1), :].reshape(N_EXPERTS * cap, D)
            y = _scatter_rows(slottp_ref[...], ys, step * per + r, cap)
            emit(rows, x_ref[rows, :] + gate * y, True)

    @pl.when(step >= PROMPT_STEPS)
    def _():
        ys = ys_ref[...].reshape(N_EXPERTS * CAP_SAMPLE, D)
        y = _scatter_rows(slotts_ref[...], ys, step - PROMPT_STEPS, CAP_SAMPLE)
        emit(slice(0, STEP_TOK), x_ref[...] + gate * y, False)


def _combine(slotts, ys, x, mod, g_next, mod_next=None):
    final = mod_next is None
    slott_p, slott_s = slotts
    row = lambda i: (i, 0)
    cond = lambda i: (_cond_index(i, STEP_TOK), 0, 0)
    in_specs = [pl.BlockSpec(slott_p.shape, lambda i: (0, 0)),
                pl.BlockSpec(slott_s.shape, lambda i: (0, 0)),
                pl.BlockSpec((N_EXPERTS, STEP_SLOTS, D), lambda i: (0, i, 0)),
                pl.BlockSpec((STEP_TOK, D), row),
                pl.BlockSpec((1, 6, D), cond)]
    args = [slott_p, slott_s, ys, x, mod]
    if final:
        in_specs.append(pl.BlockSpec((1, D), lambda i: (0, 0)))
        args.append(g_next.reshape(1, D))
        out_specs = [pl.BlockSpec((STEP_TOK, D), lambda i: (jnp.minimum(i, PROMPT_STEPS - 1), 0)),
                     pl.BlockSpec((STEP_TOK, D), lambda i: (jnp.maximum(i - PROMPT_STEPS, 0), 0))]
        out_shape = [jax.ShapeDtypeStruct((N_PROMPT_TOK, D), F32),
                     jax.ShapeDtypeStruct((N_TOK - N_PROMPT_TOK, D), F32)]
    else:
        in_specs += [pl.BlockSpec((1, 6, D), cond), pl.BlockSpec((1, D), lambda i: (0, 0))]
        args += [mod_next, g_next.reshape(1, D)]
        out_specs = [pl.BlockSpec((STEP_TOK, D), row)] * 2
        out_shape = [jax.ShapeDtypeStruct((N_TOK, D), F32), jax.ShapeDtypeStruct((N_TOK, D), BF16)]
    return pl.pallas_call(
        functools.partial(_combine_kernel, final=final),
        grid=(N_TOK // STEP_TOK,),
        in_specs=in_specs, out_specs=out_specs, out_shape=out_shape,
        compiler_params=_params(1), name="moe_combine",
    )(*args)


def _moe(x, h2, aff, mod, moe_w, layer, g_next, mod_next):
    slot_p, slott_p = _route(aff, PROMPT)
    slot_s, slott_s = _route(aff, SAMPLE)
    xs, vals = _gather(h2, aff, (slot_p, slot_s))
    ys = _ffn(xs, vals, *moe_w, layer)
    return _combine((slott_p, slott_s), ys, x, mod, g_next, mod_next)


def _block_diag_tiles(w):
    g = w.reshape(ML_INNER // ML_TILE, ML_TILE // 4, 4, 4)
    eye = jnp.eye(ML_TILE // 4, dtype=w.dtype)
    return jnp.einsum("gnio,nm->gnimo", g, eye).reshape(ML_INNER // ML_TILE, ML_TILE, ML_TILE)


def _gate_weights(w_if, b_if):
    H = ML_HEADS
    cols = jnp.stack([w_if[0][:, :H], w_if[0][:, H:], w_if[1][:, :H], w_if[1][:, H:]], axis=-1)
    cols = jnp.pad(cols, ((0, 0), (0, 0), (0, 4))).reshape(3 * ML_INNER, 8 * H)
    w = jnp.pad(cols, ((0, 0), (0, LANES - 8 * H))).astype(BF16).reshape(3, ML_INNER // ML_TILE, ML_TILE, LANES)
    b = jnp.stack([b_if[0][:H], b_if[0][H:], b_if[1][:H], b_if[1][H:]], axis=-1)
    b = jnp.pad(jnp.pad(b, ((0, 0), (0, 4))).reshape(1, 8 * H), ((0, 0), (0, LANES - 8 * H)))
    return w, b


def kernel(x_prompt, x_sample, cache_k, cache_v, state_C, state_n, state_m, c, c_ctx, w_ada, b_ada, g_norm1, g_norm2, attn_wq, attn_wk, attn_wv, attn_wo, attn_gq, attn_gk, ml_w_up, ml_conv, ml_wq, ml_wk, ml_wv, ml_w_if, ml_b_if, ml_w_o, ml_g_out, ml_skip, ml_w_down, cv_w_pw1, cv_b_pw1, cv_w_dw, cv_b_dw, cv_g_ln, cv_b_ln, cv_w_pw2, cv_b_pw2, moe_router, moe_w_gate, moe_w_up, moe_w_down, g_final):
    cond = jnp.zeros((16, D), F32).at[0].set(c_ctx).at[1:1 + N_SAMPLE_REQ].set(c)
    mods = _ada_table(cond, w_ada, b_ada).reshape(DEPTH, 16, 6, D)
    cache_k = cache_k.reshape(N_SAMPLE_REQ, -1, PAST_LEN, N_KV * HEAD_DIM)
    cache_v = cache_v.reshape(N_SAMPLE_REQ, -1, PAST_LEN, N_KV * HEAD_DIM)

    x, h = _prenorm(x_prompt.reshape(N_PROMPT_TOK, D), x_sample.reshape(-1, D), mods[0], g_norm1[0])
    new_k, new_v = [], []
    for layer in range(DEPTH):
        kind, j = layer % 3, layer // 3
        mod = mods[layer]
        wr = jnp.pad(moe_router[layer], ((0, 0), (0, LANES - N_EXPERTS)))
        if kind == 0:
            wqkv = jnp.concatenate([attn_wq[j], attn_wk[j], attn_wv[j]], axis=1).astype(BF16)
            q, k, v, k_tok, v_tok = _attn_qkv(h, wqkv, attn_gq[j], attn_gk[j])
            new_k.append(k_tok[:N_PROMPT_TOK].reshape(N_PROMPT_REQ, PROMPT_LEN, N_KV, HEAD_DIM))
            new_v.append(v_tok[:N_PROMPT_TOK].reshape(N_PROMPT_REQ, PROMPT_LEN, N_KV, HEAD_DIM))
            o = _attention(q, k, v, cache_k, cache_v, j)
            x, h2, aff = _dense_out(o, attn_wo[j].astype(BF16), x, mod, g_norm2[layer], wr)
        elif kind == 1:
            wup = ml_w_up[j].astype(BF16)
            wqkv = jnp.stack([_block_diag_tiles(ml_wq[j]), _block_diag_tiles(ml_wk[j]),
                              _block_diag_tiles(ml_wv[j])], axis=1).astype(BF16)
            wif, bif = _gate_weights(ml_w_if[j], ml_b_if[j])
            xc, q, k, v, gcol, grow = _ml_in(h, wup[:, :ML_INNER], ml_conv[j], wqkv, wif, bif)
            h0, h1, new_c, new_n, new_m = _ml_scan(q, k, v, gcol, grow, state_C, state_n, state_m)
            woz = jnp.stack([ml_w_o[j][0].astype(BF16), ml_w_o[j][1].astype(BF16), wup[:, ML_INNER:]])
            x, h2, aff = _ml_out(h, h0, h1, xc, woz, ml_g_out[j], ml_skip[j], ml_w_down[j].astype(BF16),
                                 x, mod, g_norm2[layer], wr)
        else:
            u = _conf_in(h, cv_w_pw1[j].astype(BF16), cv_b_pw1[j], cv_w_dw[j], cv_b_dw[j])
            x, h2, aff = _dense_out(u, cv_w_pw2[j].astype(BF16), x, mod, g_norm2[layer], wr,
                                    bias=cv_b_pw2[j], ln=(cv_g_ln[j], cv_b_ln[j]))
        moe_w = (moe_w_gate, moe_w_up, moe_w_down)
        if layer + 1 < DEPTH:
            x, h = _moe(x, h2, aff, mod, moe_w, layer, g_norm1[layer + 1], mods[layer + 1])
        else:
            y_prompt, y_sample = _moe(x, h2, aff, mod, moe_w, layer, g_final, None)
    return (y_prompt.reshape(N_PROMPT_REQ, PROMPT_LEN, D), y_sample.reshape(N_SAMPLE_REQ, SAMPLE_LEN, D),
            jnp.stack(new_k, axis=1), jnp.stack(new_v, axis=1), new_c, new_n, new_m)
```

```python
import functools
import math

import jax
import jax.numpy as jnp
from jax import lax
from jax.experimental import pallas as pl
from jax.experimental.pallas import tpu as pltpu

F32 = jnp.float32
BF16 = jnp.bfloat16

D = 1024
N_PROMPT_REQ, PROMPT_LEN = 32, 256
N_SAMPLE_REQ, SAMPLE_LEN = 8, 1024
N_PROMPT_TOK = N_PROMPT_REQ * PROMPT_LEN
N_TOK = N_PROMPT_TOK + N_SAMPLE_REQ * SAMPLE_LEN
DEPTH = 4
GRID_W = 64
N_HEADS, N_KV, HEAD_DIM = 16, 4, 64
PAST_LEN = 512
ROPE_THETA = 10000.0
ML_INNER, ML_HEADS, ML_DH = 2048, 4, 512
ML_CONV, ML_CHUNK = 5, 256
ML_TILE = 256
CONV_WIDTH = 31
CONV_HALO = 16
N_EXPERTS = 16
CAP_PROMPT = 2 * PROMPT_LEN // N_EXPERTS
CAP_SAMPLE = 2 * SAMPLE_LEN // N_EXPERTS
ROWS_PER_EXPERT = N_PROMPT_REQ * CAP_PROMPT + N_SAMPLE_REQ * CAP_SAMPLE
EPS = 1e-6
LANES = 128
VMEM_LIMIT = 56 * 1024 * 1024
SUB_ROWS = 256
STEP_TOK = SAMPLE_LEN
PROMPT_STEPS = N_PROMPT_TOK // STEP_TOK
STEP_SLOTS = CAP_SAMPLE

PROMPT = dict(nreq=N_PROMPT_REQ, n=PROMPT_LEN, cap=CAP_PROMPT, tok0=0)
SAMPLE = dict(nreq=N_SAMPLE_REQ, n=SAMPLE_LEN, cap=CAP_SAMPLE, tok0=N_PROMPT_TOK)


def _params(n_axes):
    return pltpu.CompilerParams(dimension_semantics=("arbitrary",) * n_axes,
                                vmem_limit_bytes=VMEM_LIMIT)


def _cond_index(i, tm):
    npt = N_PROMPT_TOK // tm
    return jnp.where(i < npt, 0, 1 + (i - npt) // (SAMPLE_LEN // tm))


def _row_slices(total, size):
    return [slice(r, r + size) for r in range(0, total, size)]


def _sigmoid(x):
    return 1.0 / (1.0 + jnp.exp(-x))


def _silu(x):
    return x * _sigmoid(x)


def _dot(a, b):
    return jnp.dot(a, b, preferred_element_type=F32)


def _dot_nt(a, b):
    return lax.dot_general(a, b, (((1,), (1,)), ((), ())), preferred_element_type=F32)


def _dot_tn(a, b):
    return lax.dot_general(a, b, (((0,), (0,)), ((), ())), preferred_element_type=F32)


def _split2(x):
    hi = x.astype(BF16)
    return hi, (x - hi.astype(F32)).astype(BF16)


def _split3(x):
    p1 = x.astype(BF16)
    r1 = x - p1.astype(F32)
    p2 = r1.astype(BF16)
    p3 = (r1 - p2.astype(F32)).astype(BF16)
    return p1, p2, p3


def _modulated_norm(x, g, shift, scale):
    ms = jnp.mean(x * x, axis=1, keepdims=True)
    return x * lax.rsqrt(ms + EPS) * g * (1.0 + scale) + shift


def _ada_kernel(c_ref, w_ref, b_ref, o_ref):
    s = _silu(c_ref[...]).astype(BF16)
    o_ref[0] = _dot(s, w_ref[0].astype(BF16)) + b_ref[0]


def _ada_table(cond, w_ada, b_ada):
    return pl.pallas_call(
        _ada_kernel,
        grid=(DEPTH, 6),
        in_specs=[pl.BlockSpec((16, D), lambda l, j: (0, 0)),
                  pl.BlockSpec((1, D, D), lambda l, j: (l, 0, j)),
                  pl.BlockSpec((1, 1, D), lambda l, j: (l, 0, j))],
        out_specs=pl.BlockSpec((1, 16, D), lambda l, j: (l, 0, j)),
        out_shape=jax.ShapeDtypeStruct((DEPTH, 16, 6 * D), F32),
        compiler_params=_params(2), name="ada_table",
    )(cond, w_ada, b_ada.reshape(DEPTH, 1, 6 * D))


def _prenorm_kernel(xp_ref, xs_ref, mod_ref, g_ref, x_ref, h_ref, *, npt):
    m = mod_ref[0]

    def emit(src_ref):
        x = src_ref[...]
        x_ref[...] = x
        h_ref[...] = _modulated_norm(x, g_ref[...], m[0:1], m[1:2]).astype(BF16)

    @pl.when(pl.program_id(0) < npt)
    def _():
        emit(xp_ref)

    @pl.when(pl.program_id(0) >= npt)
    def _():
        emit(xs_ref)


def _prenorm(x_prompt, x_sample, mod, g):
    tm = 512
    npt = N_PROMPT_TOK // tm
    row = lambda i: (i, 0)
    return pl.pallas_call(
        functools.partial(_prenorm_kernel, npt=npt),
        grid=(N_TOK // tm,),
        in_specs=[pl.BlockSpec((tm, D), lambda i: (jnp.minimum(i, npt - 1), 0)),
                  pl.BlockSpec((tm, D), lambda i: (jnp.maximum(i - npt, 0), 0)),
                  pl.BlockSpec((1, 6, D), lambda i: (_cond_index(i, tm), 0, 0)),
                  pl.BlockSpec((1, D), lambda i: (0, 0))],
        out_specs=[pl.BlockSpec((tm, D), row), pl.BlockSpec((tm, D), row)],
        out_shape=[jax.ShapeDtypeStruct((N_TOK, D), F32), jax.ShapeDtypeStruct((N_TOK, D), BF16)],
        compiler_params=_params(1), name="prenorm",
    )(x_prompt, x_sample, mod, g.reshape(1, D))


def _mixer_epilogue(o, rows, x_ref, mod_ref, g2_ref, wr_ref, xo_ref, h2_ref, aff_ref):
    m = mod_ref[0]
    x_new = x_ref[rows, :] + m[2:3] * o
    xo_ref[rows, :] = x_new
    h2 = _modulated_norm(x_new, g2_ref[...], m[3:4], m[4:5])
    hh = h2.astype(BF16)
    h2_ref[rows, :] = hh
    hl = (h2 - hh.astype(F32)).astype(BF16)
    wr = wr_ref[...]
    wh = wr.astype(BF16)
    wl = (wr - wh.astype(F32)).astype(BF16)
    both = _dot(hh, jnp.concatenate([wh, wl], axis=1))
    logits = both[:, :LANES] + both[:, LANES:] + _dot(hl, wh)
    lt = logits.T[:N_EXPERTS]
    e = jnp.exp(lt - jnp.max(lt, axis=0, keepdims=True))
    aff_ref[:, rows] = e / jnp.sum(e, axis=0, keepdims=True)


def _epilogue_specs(tm):
    in_specs = [pl.BlockSpec((tm, D), lambda i: (i, 0)),
                pl.BlockSpec((1, 6, D), lambda i: (_cond_index(i, tm), 0, 0)),
                pl.BlockSpec((1, D), lambda i: (0, 0)),
                pl.BlockSpec((D, LANES), lambda i: (0, 0))]
    out_specs = [pl.BlockSpec((tm, D), lambda i: (i, 0)),
                 pl.BlockSpec((tm, D), lambda i: (i, 0)),
                 pl.BlockSpec((N_EXPERTS, tm), lambda i: (0, i))]
    out_shape = [jax.ShapeDtypeStruct((N_TOK, D), F32),
                 jax.ShapeDtypeStruct((N_TOK, D), BF16),
                 jax.ShapeDtypeStruct((N_EXPERTS, N_TOK), F32)]
    return in_specs, out_specs, out_shape


def _dense_out_kernel(*refs, layer_norm):
    a_ref, w_ref = refs[:2]
    rest = refs[5:] if layer_norm else refs[2:]
    for rows in _row_slices(a_ref.shape[0], SUB_ROWS):
        if layer_norm:
            b_ref, gl_ref, bl_ref = refs[2:5]
            u = a_ref[rows, :]
            mu = jnp.mean(u, axis=1, keepdims=True)
            uc = u - mu
            y = uc * lax.rsqrt(jnp.mean(uc * uc, axis=1, keepdims=True) + EPS)
            a = _silu(y * gl_ref[...] + bl_ref[...]).astype(BF16)
            o = _dot(a, w_ref[...]) + b_ref[...]
        else:
            o = _dot(a_ref[rows, :], w_ref[...])
        _mixer_epilogue(o, rows, *rest)


def _dense_out(a, w, x, mod, g2, wr, bias=None, ln=None):
    tm = 512
    e_in, e_out, e_shape = _epilogue_specs(tm)
    in_specs = [pl.BlockSpec((tm, D), lambda i: (i, 0)), pl.BlockSpec((D, D), lambda i: (0, 0))]
    args = [a, w]
    if ln is not None:
        in_specs += [pl.BlockSpec((1, D), lambda i: (0, 0))] * 3
        args += [bias.reshape(1, D), ln[0].reshape(1, D), ln[1].reshape(1, D)]
    return pl.pallas_call(
        functools.partial(_dense_out_kernel, layer_norm=ln is not None),
        grid=(N_TOK // tm,),
        in_specs=in_specs + e_in, out_specs=e_out, out_shape=e_shape,
        compiler_params=_params(1), name="dense_out",
    )(*args, x, mod, g2.reshape(1, D), wr)


def _head_rms(x, gain):
    w = x.shape[1]
    head_of_lane = lax.broadcasted_iota(jnp.int32, (w, LANES), 0) // HEAD_DIM
    gather = jnp.where(head_of_lane == lax.broadcasted_iota(jnp.int32, (w, LANES), 1), 1.0, 0.0).astype(BF16)
    head_of_col = lax.broadcasted_iota(jnp.int32, (LANES, w), 1) // HEAD_DIM
    spread = jnp.where(head_of_col == lax.broadcasted_iota(jnp.int32, (LANES, w), 0), 1.0, 0.0).astype(BF16)
    ms = _dot((x * x).astype(BF16), gather) * (1.0 / HEAD_DIM)
    hi, lo = _split2(lax.rsqrt(ms + EPS))
    return x * (_dot(hi, spread) + _dot(lo, spread)) * gain


def _rope(x, c, s):
    first = (lax.broadcasted_iota(jnp.int32, (1, LANES), 1) & 31) < 16
    outs = []
    for j in range(x.shape[1] // LANES):
        sl = slice(LANES * j, LANES * (j + 1))
        seg = x[:, sl]
        partner = jnp.where(first, pltpu.roll(seg, LANES - 16, 1), pltpu.roll(seg, 16, 1))
        outs.append(seg * c[:, sl] + partner * s[:, sl])
    return jnp.concatenate(outs, axis=1)


def _spread_kv(x, ones):
    lane = lax.broadcasted_iota(jnp.int32, (1, LANES), 1)
    low = lane < HEAD_DIM
    outs = []
    for kv in range(N_KV):
        pair = x[:, LANES * (kv // 2):LANES * (kv // 2 + 1)]
        swapped = pltpu.roll(pair, HEAD_DIM, 1)
        first, second = (swapped, pair) if kv % 2 else (pair, swapped)
        first = jnp.where(low, first, 1.0 if ones else 0.0)
        second = jnp.where(low, 1.0 if ones else 0.0, second)
        if ones:
            first = jnp.where(lane > HEAD_DIM, 0.0, first)
            second = jnp.where(jnp.logical_and(lane > 0, low), 0.0, second)
        outs += [first, second]
    return jnp.concatenate(outs, axis=1).astype(BF16)


def _qkv_kernel(h_ref, w_ref, gq_ref, gk_ref, ct_ref, st_ref, q_ref, k_ref, v_ref, kt_ref, vt_ref):
    nq, nk = N_HEADS * HEAD_DIM, N_KV * HEAD_DIM
    for rows in _row_slices(h_ref.shape[0], SUB_ROWS):
        qkv = _dot(h_ref[rows, :], w_ref[...])
        q = _head_rms(qkv[:, :nq], gq_ref[...])
        k = _head_rms(qkv[:, nq:nq + nk], gk_ref[...])
        v = qkv[:, nq + nk:]
        kt_ref[rows, :] = k
        vt_ref[rows, :] = v
        c, s = ct_ref[rows, :], st_ref[rows, :]
        q_ref[rows, :] = (_rope(q, c, s) * (HEAD_DIM ** -0.5 * math.log2(math.e))).astype(BF16)
        k_ref[rows, :] = _spread_kv(_rope(k, c[:, :nk], s[:, :nk]), ones=False)
        v_ref[rows, :] = _spread_kv(v, ones=True)


def _rope_tables(tm):
    n = SAMPLE_LEN
    row = jnp.repeat(jnp.arange(n // GRID_W), GRID_W).astype(F32)
    col = jnp.tile(jnp.arange(GRID_W), n // GRID_W).astype(F32)
    nf = HEAD_DIM // 4
    inv = ROPE_THETA ** (-jnp.arange(nf, dtype=F32) / nf)
    ar, ac = row[:, None] * inv, col[:, None] * inv
    cos = jnp.concatenate([jnp.cos(ar), jnp.cos(ar), jnp.cos(ac), jnp.cos(ac)], axis=1)
    sin = jnp.concatenate([-jnp.sin(ar), jnp.sin(ar), -jnp.sin(ac), jnp.sin(ac)], axis=1)
    cos = jnp.concatenate([jnp.ones((tm, HEAD_DIM), F32), cos], axis=0)
    sin = jnp.concatenate([jnp.zeros((tm, HEAD_DIM), F32), sin], axis=0)
    return jnp.tile(cos, (1, N_HEADS)), jnp.tile(sin, (1, N_HEADS))


def _attn_qkv(h, wqkv, gq, gk):
    tm = 512
    npt = N_PROMPT_TOK // tm
    ct, st = _rope_tables(tm)
    nq, nk = N_HEADS * HEAD_DIM, N_KV * HEAD_DIM

    def tab(i):
        return (jnp.where(i < npt, 0, 1 + (i - npt) % (SAMPLE_LEN // tm)), 0)

    row = lambda i: (i, 0)
    const = lambda i: (0, 0)
    return pl.pallas_call(
        _qkv_kernel,
        grid=(N_TOK // tm,),
        in_specs=[pl.BlockSpec((tm, D), row), pl.BlockSpec((D, nq + 2 * nk), const),
                  pl.BlockSpec((1, nq), const), pl.BlockSpec((1, nk), const),
                  pl.BlockSpec((tm, nq), tab), pl.BlockSpec((tm, nq), tab)],
        out_specs=[pl.BlockSpec((tm, nq), row), pl.BlockSpec((tm, 4 * nk), row), pl.BlockSpec((tm, 4 * nk), row),
                   pl.BlockSpec((tm, nk), row), pl.BlockSpec((tm, nk), row)],
        out_shape=[jax.ShapeDtypeStruct((N_TOK, nq), BF16),
                   jax.ShapeDtypeStruct((N_TOK, 4 * nk), BF16),
                   jax.ShapeDtypeStruct((N_TOK, 4 * nk), BF16),
                   jax.ShapeDtypeStruct((N_TOK, nk), F32),
                   jax.ShapeDtypeStruct((N_TOK, nk), F32)],
        compiler_params=_params(1), name="attn_qkv",
    )(h, wqkv, jnp.tile(gq, N_HEADS).reshape(1, nq), jnp.tile(gk, N_KV).reshape(1, nk), ct, st)


def _attend(q_ref, k_ref, v_ref, o_ref, cache, row0, n):
    qb = 256
    low = lax.broadcasted_iota(jnp.int32, (1, LANES), 1) < HEAD_DIM
    for kv in range(N_KV):
        base = 2 * LANES * kv
        operands = []
        for half in range(2):
            sl = slice(base + LANES * half, base + LANES * (half + 1))
            k, v = k_ref[row0:row0 + n, sl], v_ref[row0:row0 + n, sl]
            if cache is not None:
                k = jnp.concatenate([k, cache[0][:, sl]], axis=0)
                v = jnp.concatenate([v, cache[1][:, sl]], axis=0)
            operands.append((k, v))

        def block(i, carry, base=base, operands=operands):
            r0 = row0 + i * qb if isinstance(i, int) else pl.multiple_of(row0 + i * qb, qb)
            q = jnp.concatenate([q_ref[pl.ds(r0, qb), base:base + LANES],
                                 q_ref[pl.ds(r0, qb), base + LANES:base + 2 * LANES]], axis=0)
            outs = []
            for half, (k, v) in enumerate(operands):
                s = _dot_nt(q, k)
                p = jnp.exp2(s - jnp.max(s, axis=1, keepdims=True))
                pv = _dot(p.astype(BF16), v)
                sum_col = 0 if half else HEAD_DIM
                outs.append(pv / pv[:, sum_col:sum_col + 1])
            o = jnp.where(low, outs[0], outs[1]).astype(BF16)
            o_ref[pl.ds(r0, qb), base:base + LANES] = o[:qb]
            o_ref[pl.ds(r0, qb), base + LANES:base + 2 * LANES] = o[qb:]
            return carry

        if n == qb:
            block(0, 0)
        else:
            lax.fori_loop(0, n // qb, block, 0, unroll=2)


def _attn_kernel(q_ref, k_ref, v_ref, ck_ref, cv_ref, o_ref):
    step = pl.program_id(0)

    @pl.when(step < PROMPT_STEPS)
    def _():
        for r in range(STEP_TOK // PROMPT_LEN):
            _attend(q_ref, k_ref, v_ref, o_ref, None, r * PROMPT_LEN, PROMPT_LEN)

    @pl.when(step >= PROMPT_STEPS)
    def _():
        cache = (_spread_kv(ck_ref[0, 0], ones=False), _spread_kv(cv_ref[0, 0], ones=True))
        _attend(q_ref, k_ref, v_ref, o_ref, cache, 0, SAMPLE_LEN)


def _attention(q, k, v, cache_k, cache_v, layer_j):
    spec = pl.BlockSpec((STEP_TOK, D), lambda i: (i, 0))
    cspec = pl.BlockSpec((1, 1, PAST_LEN, N_KV * HEAD_DIM),
                         lambda i: (jnp.maximum(i - PROMPT_STEPS, 0), layer_j, 0, 0))
    return pl.pallas_call(
        _attn_kernel,
        grid=(N_TOK // STEP_TOK,),
        in_specs=[spec, spec, spec, cspec, cspec], out_specs=spec,
        out_shape=jax.ShapeDtypeStruct((N_TOK, D), BF16),
        compiler_params=_params(1), name="attention",
    )(q, k, v, cache_k, cache_v)


def _conv_fill(pad_ref, x, is_prompt):
    nslab, nblk = pad_ref.shape[0], pad_ref.shape[1]
    zeros = jnp.zeros((CONV_HALO, LANES), F32)
    top, bot = slice(0, CONV_HALO), slice(CONV_HALO + SUB_ROWS, 2 * CONV_HALO + SUB_ROWS)
    for s in range(nslab):
        for j in range(nblk):
            lanes = slice(LANES * j, LANES * (j + 1))
            pad_ref[s, j, CONV_HALO:CONV_HALO + SUB_ROWS, :] = x[s * SUB_ROWS:(s + 1) * SUB_ROWS, lanes]

    @pl.when(is_prompt)
    def _():
        for s in range(nslab):
            for j in range(nblk):
                pad_ref[s, j, top, :] = zeros
                pad_ref[s, j, bot, :] = zeros

    @pl.when(jnp.logical_not(is_prompt))
    def _():
        for s in range(nslab):
            for j in range(nblk):
                lanes = slice(LANES * j, LANES * (j + 1))
                r0 = s * SUB_ROWS
                pad_ref[s, j, top, :] = x[r0 - CONV_HALO:r0, lanes] if s else zeros
                pad_ref[s, j, bot, :] = (x[r0 + SUB_ROWS:r0 + SUB_ROWS + CONV_HALO, lanes]
                                         if s + 1 < nslab else zeros)


def _conv_rows(pad_ref, w, width, s):
    cols = []
    for j in range(pad_ref.shape[1]):
        acc = None
        for d in range(width):
            tap = pad_ref[s, j, pl.ds(CONV_HALO + d - width // 2, SUB_ROWS), :] * w[d:d + 1, LANES * j:LANES * (j + 1)]
            acc = tap if acc is None else acc + tap
        cols.append(acc)
    return cols[0] if len(cols) == 1 else jnp.concatenate(cols, axis=1)


def _log_sigmoid(x):
    return jnp.minimum(x, 0.0) - jnp.log1p(jnp.exp(-jnp.abs(x)))


def _ml_in_kernel(h_ref, wup_ref, wc_ref, wqkv_ref, wif_ref, bif_ref,
                  xc_ref, q_ref, k_ref, v_ref, gcol_ref, grow_ref, pad_ref, acc_ref):
    g = pl.program_id(1)
    xm = _dot(h_ref[...], wup_ref[...])
    _conv_fill(pad_ref, xm, pl.program_id(0) < PROMPT_STEPS)
    parts = []
    for s, rows in enumerate(_row_slices(xm.shape[0], SUB_ROWS)):
        xcb = _silu(_conv_rows(pad_ref, wc_ref[...], ML_CONV, s)).astype(BF16)
        xc_ref[rows, :] = xcb
        xmb = xm[rows].astype(BF16)
        part = None
        for t in range(wqkv_ref.shape[0]):
            cols = slice(ML_TILE * t, ML_TILE * (t + 1))
            q = _dot(xcb[:, cols], wqkv_ref[t, 0]).astype(BF16)
            k = (_dot(xcb[:, cols], wqkv_ref[t, 1]) * (ML_DH ** -0.5)).astype(BF16)
            v = _dot(xmb[:, cols], wqkv_ref[t, 2]).astype(BF16)
            q_ref[rows, cols] = q
            k_ref[rows, cols] = k
            v_ref[rows, cols] = v
            gates = _dot(q, wif_ref[0, t]) + _dot(k, wif_ref[1, t]) + _dot(v, wif_ref[2, t])
            part = gates if part is None else part + gates
        parts.append(part)
    part = jnp.concatenate(parts, axis=0)

    @pl.when(g == 0)
    def _():
        acc_ref[...] = part

    @pl.when(g > 0)
    def _():
        acc_ref[...] += part

    @pl.when(g == pl.num_programs(1) - 1)
    def _():
        gates = acc_ref[...] + bif_ref[...]
        lf = _log_sigmoid(gates)
        L = ML_CHUNK
        ti = lax.broadcasted_iota(jnp.int32, (L, L), 0)
        ui = lax.broadcasted_iota(jnp.int32, (L, L), 1)
        tri_f = jnp.where(ui <= ti, 1.0, 0.0).astype(BF16)
        tri_b = jnp.where(ui >= ti, 1.0, 0.0).astype(BF16)
        kind = lax.broadcasted_iota(jnp.int32, (1, LANES), 1) & 7
        rows = []
        for c in range(gates.shape[0] // L):
            p1, p2, p3 = _split3(lf[c * L:(c + 1) * L])
            bf = _dot(tri_f, p1) + _dot(tri_f, p2) + _dot(tri_f, p3)
            bb = _dot(tri_b, p1) + _dot(tri_b, p2) + _dot(tri_b, p3)
            rows.append(jnp.where(kind == 1, bf, jnp.where(kind == 3, bb, gates[c * L:(c + 1) * L])))
        out = jnp.concatenate(rows, axis=0)
        gcol_ref[...] = out
        grow_ref[...] = out.T


def _ml_in(h, wup, wconv, wqkv, wif, bif):
    tm, cg = STEP_TOK, 512
    tiles = cg // ML_TILE
    blk = lambda i, g: (i, g)
    return pl.pallas_call(
        _ml_in_kernel,
        grid=(N_TOK // tm, ML_INNER // cg),
        in_specs=[pl.BlockSpec((tm, D), lambda i, g: (i, 0)),
                  pl.BlockSpec((D, cg), lambda i, g: (0, g)),
                  pl.BlockSpec((ML_CONV, cg), lambda i, g: (0, g)),
                  pl.BlockSpec((tiles, 3, ML_TILE, ML_TILE), lambda i, g: (g, 0, 0, 0)),
                  pl.BlockSpec((3, tiles, ML_TILE, LANES), lambda i, g: (0, g, 0, 0)),
                  pl.BlockSpec((1, LANES), lambda i, g: (0, 0))],
        out_specs=[pl.BlockSpec((tm, cg), blk)] * 4 + [
            pl.BlockSpec((tm, LANES), lambda i, g: (i, 0)),
            pl.BlockSpec((LANES, tm), lambda i, g: (0, i))],
        out_shape=[jax.ShapeDtypeStruct((N_TOK, ML_INNER), BF16),
                   jax.ShapeDtypeStruct((N_TOK, ML_INNER), BF16),
                   jax.ShapeDtypeStruct((N_TOK, ML_INNER), BF16),
                   jax.ShapeDtypeStruct((N_TOK, ML_INNER), BF16),
                   jax.ShapeDtypeStruct((N_TOK, LANES), F32),
                   jax.ShapeDtypeStruct((LANES, N_TOK), F32)],
        scratch_shapes=[pltpu.VMEM((tm // SUB_ROWS, cg // LANES, SUB_ROWS + 2 * CONV_HALO, LANES), F32),
                        pltpu.VMEM((tm, LANES), F32)],
        compiler_params=_params(2), name="mlstm_in",
    )(h, wup, wconv, wqkv, wif, bif)


def _scan_request(q_ref, k_ref, v_ref, gcol_ref, grow_ref, h_refs, hh, row0, n, carried, emit_state):
    L = ML_CHUNK
    nc = n // L
    lane = lax.broadcasted_iota(jnp.int32, (1, LANES), 1)
    ti = lax.broadcasted_iota(jnp.int32, (L, L), 0)
    si = lax.broadcasted_iota(jnp.int32, (L, L), 1)
    qk_single = None
    for d in range(2):
        if carried is not None:
            c0_ref, n0_ref, m0, c_sc, qk_sc = carried
            c_sc[...] = c0_ref[0, 0, d, 0]
            nvec = n0_ref[0, d, 0]
            m = jnp.full((1, 1), m0[d], F32)
        else:
            m = jnp.zeros((1, 1), F32)
        order = range(nc - 1, -1, -1) if d else range(nc)
        for step, c in enumerate(order):
            rows = slice(row0 + c * L, row0 + (c + 1) * L)
            qc, kc, vc = q_ref[rows, :], k_ref[rows, :], v_ref[rows, :]
            if nc == 1:
                if qk_single is None:
                    qk_single = _dot_nt(qc, kc)
                qk = qk_single
            elif d == 0:
                qk = _dot_nt(qc, kc)
                qk_sc[c] = qk
            else:
                qk = qk_sc[c]
            gc = gcol_ref[rows, :]

            def col(j, gc=gc):
                return jnp.sum(jnp.where(lane == 8 * hh + j, gc, 0.0), axis=1, keepdims=True)

            i_col, b_col = col(2 * d), col(2 * d + 1)
            i_row = grow_ref[2 * d:2 * d + 1, rows]
            b_row = grow_ref[2 * d + 1:2 * d + 2, rows]
            mask = (si >= ti) if d else (si <= ti)
            dm = jnp.where(mask, b_col - b_row + i_row, -jnp.inf)
            mt = jnp.maximum(b_col + m, jnp.max(dm, axis=1, keepdims=True))
            s = qk * jnp.exp(dm - mt)
            den = jnp.sum(s, axis=1, keepdims=True)
            num = _dot(s.astype(BF16), vc)
            if carried is not None:
                w_in = jnp.exp(b_col + m - mt)
                num = num + w_in * _dot_nt(qc, c_sc[...].astype(BF16))
                den = den + w_in * jnp.sum(qc.astype(F32) * nvec, axis=1, keepdims=True)
            h_refs[d][rows, :] = (num / jnp.maximum(jnp.abs(den), jnp.exp(-mt))).astype(BF16)
            if carried is not None and step == nc - 1:
                continue
            edge = 0 if d else L - 1
            b_last = b_row[:, edge:edge + 1]
            m_new = mt[edge:edge + 1, :]
            kw = kc.astype(F32) * jnp.exp(b_last - b_col + i_col - m_new)
            upd = _dot_tn(vc, kw.astype(BF16))
            n_upd = jnp.sum(kw, axis=0, keepdims=True)
            if carried is not None:
                w_state = jnp.exp(b_last + m - m_new)
                c_sc[...] = w_state * c_sc[...] + upd
                nvec = w_state * nvec + n_upd
            else:
                emit_state(d, upd, n_upd, m_new)
            m = m_new


def _scan_kernel(q_ref, k_ref, v_ref, gcol_ref, grow_ref, c0_ref, n0_ref, m0_ref,
                 h0_ref, h1_ref, cn_ref, nn_ref, mn_ref, c_sc, qk_sc):
    hh = pl.program_id(0)
    step = pl.program_id(1)
    seqs = (q_ref, k_ref, v_ref, gcol_ref, grow_ref, (h0_ref, h1_ref), hh)

    @pl.when(step < PROMPT_STEPS)
    def _():
        mn_ref[...] = jnp.zeros(mn_ref.shape, F32)
        for r in range(STEP_TOK // PROMPT_LEN):
            def emit_state(d, c_new, n_new, m_new, r=r):
                cn_ref[r, 0, d, 0] = c_new
                nn_ref[r, d, 0] = n_new
                mn_ref[r, 0, d:d + 1, :] = jnp.broadcast_to(m_new, (1, LANES))

            _scan_request(*seqs, r * PROMPT_LEN, PROMPT_LEN, None, emit_state)

    @pl.when(step >= PROMPT_STEPS)
    def _():
        b = step - PROMPT_STEPS
        m0 = [m0_ref[b, d * ML_HEADS + hh] for d in range(2)]
        _scan_request(*seqs, 0, SAMPLE_LEN, (c0_ref, n0_ref, m0, c_sc, qk_sc), None)


def _ml_scan(q, k, v, gcol, grow, state_c, state_n, state_m):
    T = STEP_TOK
    per = T // PROMPT_LEN
    rb = lambda hh, i: (i, hh)
    sample = lambda i: jnp.maximum(i - PROMPT_STEPS, 0)
    prompt = lambda i: jnp.minimum(i, PROMPT_STEPS - 1)
    hshape = jax.ShapeDtypeStruct((N_TOK, ML_INNER), BF16)
    h0, h1, new_c, new_n, new_m = pl.pallas_call(
        _scan_kernel,
        grid=(ML_HEADS, N_TOK // T),
        in_specs=[pl.BlockSpec((T, ML_DH), rb)] * 3 + [
            pl.BlockSpec((T, LANES), lambda hh, i: (i, 0)),
            pl.BlockSpec((8, T), lambda hh, i: (hh, i)),
            pl.BlockSpec((1, 1, 2, 1, ML_DH, ML_DH), lambda hh, i: (sample(i), 0, 0, hh, 0, 0)),
            pl.BlockSpec((1, 2, 1, 1, ML_DH), lambda hh, i: (sample(i), 0, hh, 0, 0)),
            pl.BlockSpec(memory_space=pltpu.SMEM)],
        out_specs=[pl.BlockSpec((T, ML_DH), rb)] * 2 + [
            pl.BlockSpec((per, 1, 2, 1, ML_DH, ML_DH), lambda hh, i: (prompt(i), 0, 0, hh, 0, 0)),
            pl.BlockSpec((per, 2, 1, 1, ML_DH), lambda hh, i: (prompt(i), 0, hh, 0, 0)),
            pl.BlockSpec((per, 1, 8, LANES), lambda hh, i: (prompt(i), hh, 0, 0))],
        out_shape=[hshape, hshape,
                   jax.ShapeDtypeStruct((N_PROMPT_REQ, 1, 2, ML_HEADS, ML_DH, ML_DH), F32),
                   jax.ShapeDtypeStruct((N_PROMPT_REQ, 2, ML_HEADS, 1, ML_DH), F32),
                   jax.ShapeDtypeStruct((N_PROMPT_REQ, ML_HEADS, 8, LANES), F32)],
        scratch_shapes=[pltpu.VMEM((ML_DH, ML_DH), F32),
                        pltpu.VMEM((SAMPLE_LEN // ML_CHUNK, ML_CHUNK, ML_CHUNK), F32)],
        compiler_params=_params(2), name="mlstm_scan",
    )(q, k, v, gcol, grow, state_c,
      state_n.reshape(N_SAMPLE_REQ, 2, ML_HEADS, 1, ML_DH), state_m.reshape(N_SAMPLE_REQ, 2 * ML_HEADS))
    new_n = new_n.reshape(N_PROMPT_REQ, 1, 2, ML_HEADS, ML_DH)
    new_m = jnp.transpose(new_m[:, :, :2, 0], (0, 2, 1)).reshape(N_PROMPT_REQ, 1, 2, ML_HEADS)
    return h0, h1, new_c, new_n, new_m


def _ml_out_kernel(h_ref, h0_ref, h1_ref, xc_ref, woz_ref, gout_ref, skip_ref, wd_ref, *rest):
    for rows in _row_slices(h_ref.shape[0], 2 * SUB_ROWS):
        h = h_ref[rows, :]
        acc = None
        for hd in range(ML_HEADS):
            sl = slice(ML_DH * hd, ML_DH * (hd + 1))
            o0 = _sigmoid(_dot(h, woz_ref[0, :, sl]))
            o1 = _sigmoid(_dot(h, woz_ref[1, :, sl]))
            z = _dot(h, woz_ref[2, :, sl])
            hs = o0 * h0_ref[rows, sl] + o1 * h1_ref[rows, sl]
            hn = hs * lax.rsqrt(jnp.mean(hs * hs, axis=1, keepdims=True) + EPS) * gout_ref[:, sl]
            y = ((hn + skip_ref[:, sl] * xc_ref[rows, sl]) * _silu(z)).astype(BF16)
            part = _dot(y, wd_ref[sl, :])
            acc = part if acc is None else acc + part
        _mixer_epilogue(acc, rows, *rest)


def _ml_out(h, h0, h1, xc, woz, gout, skip, wdown, x, mod, g2, wr):
    tm = 512
    e_in, e_out, e_shape = _epilogue_specs(tm)
    row = lambda i: (i, 0)
    const = lambda i: (0, 0)
    once = pl.Buffered(1)
    return pl.pallas_call(
        _ml_out_kernel,
        grid=(N_TOK // tm,),
        in_specs=[pl.BlockSpec((tm, D), row), pl.BlockSpec((tm, ML_INNER), row),
                  pl.BlockSpec((tm, ML_INNER), row), pl.BlockSpec((tm, ML_INNER), row),
                  pl.BlockSpec((3, D, ML_INNER), lambda i: (0, 0, 0), pipeline_mode=once),
                  pl.BlockSpec((1, ML_INNER), const), pl.BlockSpec((1, ML_INNER), const),
                  pl.BlockSpec((ML_INNER, D), const, pipeline_mode=once)] + e_in,
        out_specs=e_out, out_shape=e_shape,
        compiler_params=_params(1), name="mlstm_out",
    )(h, h0, h1, xc, woz, gout.reshape(1, ML_INNER), skip.reshape(1, ML_INNER), wdown,
      x, mod, g2.reshape(1, D), wr)


def _conf_in_kernel(h_ref, wa_ref, wg_ref, ba_ref, bg_ref, wdw_ref, bdw_ref, u_ref, pad_ref):
    h = h_ref[...]
    a = _dot(h, wa_ref[...]) + ba_ref[...]
    g = _dot(h, wg_ref[...]) + bg_ref[...]
    _conv_fill(pad_ref, a * _sigmoid(g), pl.program_id(0) < PROMPT_STEPS)
    for s, rows in enumerate(_row_slices(h.shape[0], SUB_ROWS)):
        u_ref[rows, :] = _conv_rows(pad_ref, wdw_ref[...], CONV_WIDTH, s) + bdw_ref[...]


def _conf_in(h, w1, b1, wdw, bdw):
    tm, cg = STEP_TOK, 256
    ngroups = D // cg
    b1 = b1.reshape(1, 2 * D)
    return pl.pallas_call(
        _conf_in_kernel,
        grid=(N_TOK // tm, ngroups),
        in_specs=[pl.BlockSpec((tm, D), lambda i, g: (i, 0)),
                  pl.BlockSpec((D, cg), lambda i, g: (0, g)),
                  pl.BlockSpec((D, cg), lambda i, g: (0, ngroups + g)),
                  pl.BlockSpec((1, cg), lambda i, g: (0, g)),
                  pl.BlockSpec((1, cg), lambda i, g: (0, ngroups + g)),
                  pl.BlockSpec((CONV_WIDTH, cg), lambda i, g: (0, g)),
                  pl.BlockSpec((1, cg), lambda i, g: (0, g))],
        out_specs=pl.BlockSpec((tm, cg), lambda i, g: (i, g)),
        out_shape=jax.ShapeDtypeStruct((N_TOK, D), F32),
        scratch_shapes=[pltpu.VMEM((tm // SUB_ROWS, cg // LANES, SUB_ROWS + 2 * CONV_HALO, LANES), F32)],
        compiler_params=_params(2), name="conformer_in",
    )(h, w1, w1, b1, b1, wdw, bdw.reshape(1, D))


def _route_kernel(aff_ref, slot_ref, slott_ref, *, nreq, n, cap):
    a = jnp.concatenate([aff_ref[:, n * r:n * (r + 1)] for r in range(nreq)], axis=0)
    capf = float(cap)

    def bisect(i, p):
        cand = p | jnp.left_shift(jnp.int32(1), 30 - i)
        cnt = jnp.sum(jnp.where(a >= pltpu.bitcast(cand, F32), 1.0, 0.0), axis=1, keepdims=True)
        return jnp.where(cnt >= capf, cand, p)

    thr = pltpu.bitcast(lax.fori_loop(0, 31, bisect, jnp.zeros((a.shape[0], 1), jnp.int32)), F32)
    gt = a > thr
    eq = a == thr
    need = capf - jnp.sum(jnp.where(gt, 1.0, 0.0), axis=1, keepdims=True)
    before = jnp.where(lax.broadcasted_iota(jnp.int32, (n, n), 0) < lax.broadcasted_iota(jnp.int32, (n, n), 1),
                       1.0, 0.0).astype(BF16)
    eq_rank = _dot(jnp.where(eq, 1.0, 0.0).astype(BF16), before)
    sel = jnp.logical_or(gt, jnp.logical_and(eq, eq_rank < need))
    pos = _dot(jnp.where(sel, 1.0, 0.0).astype(BF16), before)
    slot = jnp.where(sel, pos, -1.0)
    slot_ref[...] = slot
    slott_ref[...] = slot.T


def _route(aff, cfg):
    nreq, n, cap = cfg["nreq"], cfg["n"], cfg["cap"]
    rows = nreq * N_EXPERTS
    half = cfg["tok0"] // (nreq * n)
    return pl.pallas_call(
        functools.partial(_route_kernel, nreq=nreq, n=n, cap=cap),
        grid=(1,),
        in_specs=[pl.BlockSpec((N_EXPERTS, nreq * n), lambda i: (0, half))],
        out_specs=[pl.BlockSpec((rows, n), lambda i: (0, 0)), pl.BlockSpec((n, rows), lambda i: (0, 0))],
        out_shape=[jax.ShapeDtypeStruct((rows, n), F32), jax.ShapeDtypeStruct((n, rows), F32)],
        compiler_params=_params(1), name="moe_route",
    )(aff)


def _gather_rows(h, slot, aff, cap):
    n = slot.shape[1]
    ci = lax.broadcasted_iota(jnp.int32, (cap, n), 0).astype(F32)
    onehots, vals = [], []
    for e in range(N_EXPERTS):
        hit = slot[e:e + 1, :] == ci
        onehots.append(jnp.where(hit, 1.0, 0.0).astype(BF16))
        val = jnp.sum(jnp.where(hit, aff[e:e + 1, :], 0.0), axis=1, keepdims=True)
        vals.append(jnp.broadcast_to(val, (cap, LANES)))
    return _dot(jnp.concatenate(onehots, axis=0), h).astype(BF16), vals


def _gather_kernel(h_ref, slotp_ref, slots_ref, aff_ref, xs_ref, vals_ref):
    step = pl.program_id(0)

    @pl.when(step < PROMPT_STEPS)
    def _():
        n, cap = PROMPT_LEN, CAP_PROMPT
        for r in range(STEP_TOK // n):
            xs, vals = _gather_rows(h_ref[n * r:n * (r + 1), :], slotp_ref[N_EXPERTS * r:N_EXPERTS * (r + 1), :],
                                    aff_ref[:, n * r:n * (r + 1)], cap)
            xs_ref[:, cap * r:cap * (r + 1), :] = xs.reshape(N_EXPERTS, cap, D)
            for e in range(N_EXPERTS):
                vals_ref[e, cap * r:cap * (r + 1), :] = vals[e]

    @pl.when(step >= PROMPT_STEPS)
    def _():
        xs, vals = _gather_rows(h_ref[...], slots_ref[...], aff_ref[...], CAP_SAMPLE)
        xs_ref[...] = xs.reshape(N_EXPERTS, CAP_SAMPLE, D)
        for e in range(N_EXPERTS):
            vals_ref[e] = vals[e]


def _gather(h2, aff, slots):
    slot_p, slot_s = slots
    per = STEP_TOK // PROMPT_LEN
    return pl.pallas_call(
        _gather_kernel,
        grid=(N_TOK // STEP_TOK,),
        in_specs=[pl.BlockSpec((STEP_TOK, D), lambda i: (i, 0)),
                  pl.BlockSpec((per * N_EXPERTS, PROMPT_LEN), lambda i: (jnp.minimum(i, PROMPT_STEPS - 1), 0)),
                  pl.BlockSpec((N_EXPERTS, SAMPLE_LEN), lambda i: (jnp.maximum(i - PROMPT_STEPS, 0), 0)),
                  pl.BlockSpec((N_EXPERTS, STEP_TOK), lambda i: (0, i))],
        out_specs=[pl.BlockSpec((N_EXPERTS, STEP_SLOTS, D), lambda i: (0, i, 0)),
                   pl.BlockSpec((N_EXPERTS, STEP_SLOTS, LANES), lambda i: (0, i, 0))],
        out_shape=[jax.ShapeDtypeStruct((N_EXPERTS, ROWS_PER_EXPERT, D), BF16),
                   jax.ShapeDtypeStruct((N_EXPERTS, ROWS_PER_EXPERT, LANES), F32)],
        compiler_params=_params(1), name="moe_gather",
    )(h2, slot_p, slot_s, aff)


def _ffn_kernel(xs_ref, vals_ref, wg_ref, wu_ref, wd_ref, ys_ref):
    wg = wg_ref[0, 0].astype(BF16)
    wu = wu_ref[0, 0].astype(BF16)
    wd = wd_ref[0, 0].astype(BF16)
    for rows in _row_slices(xs_ref.shape[1], 2 * SUB_ROWS):
        xs = xs_ref[0, rows, :]
        act = (_silu(_dot(xs, wg)) * _dot(xs, wu)).astype(BF16)
        ys = _dot(act, wd)
        vals = vals_ref[0, rows, :]
        ys_ref[0, rows, :] = jnp.concatenate(
            [ys[:, LANES * j:LANES * (j + 1)] * vals for j in range(D // LANES)], axis=1).astype(BF16)


def _ffn(xs, vals, wg, wu, wd, layer):
    wspec = pl.BlockSpec((1, 1, D, D), lambda e: (layer, e, 0, 0))
    rows = lambda e: (e, 0, 0)
    return pl.pallas_call(
        _ffn_kernel,
        grid=(N_EXPERTS,),
        in_specs=[pl.BlockSpec((1, ROWS_PER_EXPERT, D), rows),
                  pl.BlockSpec((1, ROWS_PER_EXPERT, LANES), rows),
                  wspec, wspec, wspec],
        out_specs=pl.BlockSpec((1, ROWS_PER_EXPERT, D), rows),
        out_shape=jax.ShapeDtypeStruct((N_EXPERTS, ROWS_PER_EXPERT, D), BF16),
        compiler_params=_params(1), name="moe_ffn",
    )(xs, vals, wg, wu, wd)


def _scatter_rows(slott, ys, req, cap):
    st = slott.astype(BF16)
    rb, width = st.shape[1], N_EXPERTS * cap
    ri = lax.broadcasted_iota(jnp.int32, (rb, width), 0)
    ji = lax.broadcasted_iota(jnp.int32, (rb, width), 1)
    expand = jnp.where(ri == N_EXPERTS * req + (ji >> int(math.log2(cap))), 1.0, 0.0).astype(BF16)
    want = (lax.broadcasted_iota(jnp.int32, (1, width), 1) & (cap - 1)).astype(F32)
    scatter = jnp.where(_dot(st, expand) == want, 1.0, 0.0).astype(BF16)
    return _dot(scatter, ys)


def _combine_kernel(*refs, final):
    slottp_ref, slotts_ref, ys_ref, x_ref, mod_ref = refs[:5]
    step = pl.program_id(0)
    gate = mod_ref[0][5:6]
    if final:
        g_ref, yp_ref, ysm_ref = refs[5:]
    else:
        modn_ref, g_ref, xo_ref, h_ref = refs[5:]

    def emit(rows, x_new, prompt):
        if final:
            y = x_new * lax.rsqrt(jnp.mean(x_new * x_new, axis=1, keepdims=True) + EPS) * g_ref[...]
            (yp_ref if prompt else ysm_ref)[rows, :] = y
        else:
            xo_ref[rows, :] = x_new
            mn = modn_ref[0]
            h_ref[rows, :] = _modulated_norm(x_new, g_ref[...], mn[0:1], mn[1:2]).astype(BF16)

    @pl.when(step < PROMPT_STEPS)
    def _():
        n, cap = PROMPT_LEN, CAP_PROMPT
        per = STEP_TOK // n
        for r in range(per):
            rows = slice(n * r, n * (r + 1))
            ys = ys_ref[:, cap * r:cap * (r + 1), :].reshape(N_EXPERTS * cap, D)
            y = _scatter_rows(slottp_ref[...], ys, step * per + r, cap)
            emit(rows, x_ref[rows, :] + gate * y, True)

    @pl.when(step >= PROMPT_STEPS)
    def _():
        ys = ys_ref[...].reshape(N_EXPERTS * CAP_SAMPLE, D)
        y = _scatter_rows(slotts_ref[...], ys, step - PROMPT_STEPS, CAP_SAMPLE)
        emit(slice(0, STEP_TOK), x_ref[...] + gate * y, False)


def _combine(slotts, ys, x, mod, g_next, mod_next=None):
    final = mod_next is None
    slott_p, slott_s = slotts
    row = lambda i: (i, 0)
    cond = lambda i: (_cond_index(i, STEP_TOK), 0, 0)
    in_specs = [pl.BlockSpec(slott_p.shape, lambda i: (0, 0)),
                pl.BlockSpec(slott_s.shape, lambda i: (0, 0)),
                pl.BlockSpec((N_EXPERTS, STEP_SLOTS, D), lambda i: (0, i, 0)),
                pl.BlockSpec((STEP_TOK, D), row),
                pl.BlockSpec((1, 6, D), cond)]
    args = [slott_p, slott_s, ys, x, mod]
    if final:
        in_specs.append(pl.BlockSpec((1, D), lambda i: (0, 0)))
        args.append(g_next.reshape(1, D))
        out_specs = [pl.BlockSpec((STEP_TOK, D), lambda i: (jnp.minimum(i, PROMPT_STEPS - 1), 0)),
                     pl.BlockSpec((STEP_TOK, D), lambda i: (jnp.maximum(i - PROMPT_STEPS, 0), 0))]
        out_shape = [jax.ShapeDtypeStruct((N_PROMPT_TOK, D), F32),
                     jax.ShapeDtypeStruct((N_TOK - N_PROMPT_TOK, D), F32)]
    else:
        in_specs += [pl.BlockSpec((1, 6, D), cond), pl.BlockSpec((1, D), lambda i: (0, 0))]
        args += [mod_next, g_next.reshape(1, D)]
        out_specs = [pl.BlockSpec((STEP_TOK, D), row)] * 2
        out_shape = [jax.ShapeDtypeStruct((N_TOK, D), F32), jax.ShapeDtypeStruct((N_TOK, D), BF16)]
    return pl.pallas_call(
        functools.partial(_combine_kernel, final=final),
        grid=(N_TOK // STEP_TOK,),
        in_specs=in_specs, out_specs=out_specs, out_shape=out_shape,
        compiler_params=_params(1), name="moe_combine",
    )(*args)


def _moe(x, h2, aff, mod, moe_w, layer, g_next, mod_next):
    slot_p, slott_p = _route(aff, PROMPT)
    slot_s, slott_s = _route(aff, SAMPLE)
    xs, vals = _gather(h2, aff, (slot_p, slot_s))
    ys = _ffn(xs, vals, *moe_w, layer)
    return _combine((slott_p, slott_s), ys, x, mod, g_next, mod_next)


def _block_diag_tiles(w):
    g = w.reshape(ML_INNER // ML_TILE, ML_TILE // 4, 4, 4)
    eye = jnp.eye(ML_TILE // 4, dtype=w.dtype)
    return jnp.einsum("gnio,nm->gnimo", g, eye).reshape(ML_INNER // ML_TILE, ML_TILE, ML_TILE)


def _gate_weights(w_if, b_if):
    H = ML_HEADS
    cols = jnp.stack([w_if[0][:, :H], w_if[0][:, H:], w_if[1][:, :H], w_if[1][:, H:]], axis=-1)
    cols = jnp.pad(cols, ((0, 0), (0, 0), (0, 4))).reshape(3 * ML_INNER, 8 * H)
    w = jnp.pad(cols, ((0, 0), (0, LANES - 8 * H))).astype(BF16).reshape(3, ML_INNER // ML_TILE, ML_TILE, LANES)
    b = jnp.stack([b_if[0][:H], b_if[0][H:], b_if[1][:H], b_if[1][H:]], axis=-1)
    b = jnp.pad(jnp.pad(b, ((0, 0), (0, 4))).reshape(1, 8 * H), ((0, 0), (0, LANES - 8 * H)))
    return w, b


def kernel(x_prompt, x_sample, cache_k, cache_v, state_C, state_n, state_m, c, c_ctx, w_ada, b_ada, g_norm1, g_norm2, attn_wq, attn_wk, attn_wv, attn_wo, attn_gq, attn_gk, ml_w_up, ml_conv, ml_wq, ml_wk, ml_wv, ml_w_if, ml_b_if, ml_w_o, ml_g_out, ml_skip, ml_w_down, cv_w_pw1, cv_b_pw1, cv_w_dw, cv_b_dw, cv_g_ln, cv_b_ln, cv_w_pw2, cv_b_pw2, moe_router, moe_w_gate, moe_w_up, moe_w_down, g_final):
    cond = jnp.zeros((16, D), F32).at[0].set(c_ctx).at[1:1 + N_SAMPLE_REQ].set(c)
    mods = _ada_table(cond, w_ada, b_ada).reshape(DEPTH, 16, 6, D)
    cache_k = cache_k.reshape(N_SAMPLE_REQ, -1, PAST_LEN, N_KV * HEAD_DIM)
    cache_v = cache_v.reshape(N_SAMPLE_REQ, -1, PAST_LEN, N_KV * HEAD_DIM)

    x, h = _prenorm(x_prompt.reshape(N_PROMPT_TOK, D), x_sample.reshape(-1, D), mods[0], g_norm1[0])
    new_k, new_v = [], []
    for layer in range(DEPTH):
        kind, j = layer % 3, layer // 3
        mod = mods[layer]
        wr = jnp.pad(moe_router[layer], ((0, 0), (0, LANES - N_EXPERTS)))
        if kind == 0:
            wqkv = jnp.concatenate([attn_wq[j], attn_wk[j], attn_wv[j]], axis=1).astype(BF16)
            q, k, v, k_tok, v_tok = _attn_qkv(h, wqkv, attn_gq[j], attn_gk[j])
            new_k.append(k_tok[:N_PROMPT_TOK].reshape(N_PROMPT_REQ, PROMPT_LEN, N_KV, HEAD_DIM))
            new_v.append(v_tok[:N_PROMPT_TOK].reshape(N_PROMPT_REQ, PROMPT_LEN, N_KV, HEAD_DIM))
            o = _attention(q, k, v, cache_k, cache_v, j)
            x, h2, aff = _dense_out(o, attn_wo[j].astype(BF16), x, mod, g_norm2[layer], wr)
        elif kind == 1:
            wup = ml_w_up[j].astype(BF16)
            wqkv = jnp.stack([_block_diag_tiles(ml_wq[j]), _block_diag_tiles(ml_wk[j]),
                              _block_diag_tiles(ml_wv[j])], axis=1).astype(BF16)
            wif, bif = _gate_weights(ml_w_if[j], ml_b_if[j])
            xc, q, k, v, gcol, grow = _ml_in(h, wup[:, :ML_INNER], ml_conv[j], wqkv, wif, bif)
            h0, h1, new_c, new_n, new_m = _ml_scan(q, k, v, gcol, grow, state_C, state_n, state_m)
            woz = jnp.stack([ml_w_o[j][0].astype(BF16), ml_w_o[j][1].astype(BF16), wup[:, ML_INNER:]])
            x, h2, aff = _ml_out(h, h0, h1, xc, woz, ml_g_out[j], ml_skip[j], ml_w_down[j].astype(BF16),
                                 x, mod, g_norm2[layer], wr)
        else:
            u = _conf_in(h, cv_w_pw1[j].astype(BF16), cv_b_pw1[j], cv_w_dw[j], cv_b_dw[j])
            x, h2, aff = _dense_out(u, cv_w_pw2[j].astype(BF16), x, mod, g_norm2[layer], wr,
                                    bias=cv_b_pw2[j], ln=(cv_g_ln[j], cv_b_ln[j]))
        moe_w = (moe_w_gate, moe_w_up, moe_w_down)
        if layer + 1 < DEPTH:
            x, h = _moe(x, h2, aff, mod, moe_w, layer, g_norm1[layer + 1], mods[layer + 1])
        else:
            y_prompt, y_sample = _moe(x, h2, aff, mod, moe_w, layer, g_final, None)
    return (y_prompt.reshape(N_PROMPT_REQ, PROMPT_LEN, D), y_sample.reshape(N_SAMPLE_REQ, SAMPLE_LEN, D),
            jnp.stack(new_k, axis=1), jnp.stack(new_v, axis=1), new_c, new_n, new_m)
```
